```python
import jax, jax.numpy as jnp
from jax import lax
import numpy as np

D_MODEL = 2048
BATCH = 2
SEQ = 8192
DEPTH = 1
DEC_BATCH = 32
DEC_SEQ = 64
PAST_LEN = 2048

CHUNK = 64
HEAD_DIM = 64
MIX_WIDTH = D_MODEL
RWKV_WIDTH = MIX_WIDTH // 2
SB_WIDTH = MIX_WIDTH - RWKV_WIDTH
RWKV_HEADS = RWKV_WIDTH // HEAD_DIM
SB_HEADS = SB_WIDTH // HEAD_DIM
DECAY_LORA = 64
AAA_LORA = 64
GATE_LORA = 160
RWKV_IN = 3 * RWKV_WIDTH + DECAY_LORA + AAA_LORA + GATE_LORA
RWKV_SPLITS = [RWKV_WIDTH, 2 * RWKV_WIDTH, 3 * RWKV_WIDTH, 3 * RWKV_WIDTH + DECAY_LORA,
               3 * RWKV_WIDTH + DECAY_LORA + AAA_LORA]
SB_IN = 3 * SB_WIDTH
N_IN = RWKV_IN + SB_IN
N_EXPERTS = 32
TOP_K = 4
D_FF = D_MODEL
SWIGLU_ALPHA = 1.702
SWIGLU_LIMIT = 7.0
Q_BLOCK = 128
EXPERT_BLOCK = 256
RMS_EPS = 1e-5
GN_EPS = 64e-5

kernel_name = 'rwkv7_stickbreak_moe_stream_step'


def rmsnorm(x, g):
    xf = x.astype(jnp.float32)
    y = xf * lax.rsqrt(jnp.mean(xf * xf, axis=-1, keepdims=True) + RMS_EPS)
    return (y * g.astype(jnp.float32)).astype(x.dtype)


def token_shift(u, prev, mu):
    shifted = jnp.concatenate([prev[:, None, :].astype(u.dtype), u[:, :-1]], axis=1)
    return u + (shifted - u) * mu


def rwkv7_scan(S0, r, decay, k, v, kk, a):
    def step(S, inp):
        r_t, w_t, k_t, v_t, kk_t, a_t = inp
        sa = jnp.einsum('bhvk,bhk->bhv', S, -kk_t)
        S = (S * w_t[:, :, None, :] + sa[..., None] * (kk_t * a_t)[:, :, None, :]
             + v_t[..., None] * k_t[:, :, None, :])
        return S, jnp.einsum('bhvk,bhk->bhv', S, r_t)
    xs = tuple(jnp.moveaxis(t, 1, 0) for t in (r, decay, k, v, kk, a))
    S, ys = lax.scan(step, S0, xs)
    return jnp.moveaxis(ys, 0, 1), S


def head_groupnorm(y, g, b):
    mean = jnp.mean(y, axis=-1, keepdims=True)
    yc = y - mean
    var = jnp.mean(yc * yc, axis=-1, keepdims=True)
    return yc * lax.rsqrt(var + GN_EPS) * g + b


def sb_attend(q, k, v, q_pos0):
    tq, tk = q.shape[1], k.shape[1]
    z = jnp.einsum('bqhd,bkhd->bhqk', q, k).astype(jnp.float32) * (HEAD_DIM ** -0.5)
    tpos = q_pos0 + jnp.arange(tq)
    mask = jnp.arange(tk)[None, :] < tpos[:, None]
    log_1mb = jnp.where(mask, jax.nn.log_sigmoid(-z), 0.0)
    between = lax.cumsum(log_1mb, axis=3, reverse=True) - log_1mb
    att = jnp.where(mask, jnp.exp(jax.nn.log_sigmoid(z) + between), 0.0)
    return jnp.einsum('bhqk,bkhd->bqhd', att.astype(v.dtype), v)


def sb_prompt(q, k, v):
    B, S, H, dh = q.shape
    nb = S // Q_BLOCK
    qb = q.reshape(B, nb, Q_BLOCK, H, dh).transpose(1, 0, 2, 3, 4)
    ob = lax.map(lambda args: sb_attend(args[0], k, v, args[1] * Q_BLOCK), (qb, jnp.arange(nb)))
    return ob.transpose(1, 0, 2, 3, 4).reshape(B, S, H, dh)


def moe_ffn(h, router_w, router_b, w1, b1, w2, b2):
    B, T, D = h.shape
    x = h.reshape(B * T, D)
    n_asg = B * T * TOP_K
    logits = (x @ router_w).astype(jnp.float32) + router_b.astype(jnp.float32)
    top_logit, top_e = lax.top_k(logits, TOP_K)
    gate = jax.nn.softmax(top_logit, axis=-1)
    flat_e = top_e.reshape(-1)
    order = jnp.argsort(flat_e)
    e_sorted = flat_e[order]
    tok_sorted = order // TOP_K
    gate_sorted = gate.reshape(-1)[order]
    counts = jnp.bincount(flat_e, length=N_EXPERTS)
    padded = (counts + EXPERT_BLOCK - 1) // EXPERT_BLOCK * EXPERT_BLOCK
    start = jnp.cumsum(counts) - counts
    pend = jnp.cumsum(padded)
    pstart = pend - padded
    dest = pstart[e_sorted] + jnp.arange(n_asg) - start[e_sorted]
    n_blocks = -(-n_asg // EXPERT_BLOCK) + N_EXPERTS
    block_start = jnp.arange(n_blocks) * EXPERT_BLOCK
    block_e = jnp.minimum(jnp.sum(block_start[:, None] >= pend[None, :], axis=1), N_EXPERTS - 1)
    xbuf = jnp.zeros((n_blocks * EXPERT_BLOCK, D), x.dtype).at[dest].set(x[tok_sorted])

    def expert_block(args):
        xb, e = args
        hm = xb @ w1[e] + b1[e]
        glu = jnp.minimum(hm[:, :D_FF], SWIGLU_LIMIT)
        lin = jnp.clip(hm[:, D_FF:], -SWIGLU_LIMIT, SWIGLU_LIMIT)
        act = glu * jax.nn.sigmoid(SWIGLU_ALPHA * glu) * (lin + 1.0)
        return act @ w2[e] + b2[e]

    ybuf = lax.map(expert_block, (xbuf.reshape(n_blocks, EXPERT_BLOCK, D), block_e))
    y_sorted = ybuf.reshape(n_blocks * EXPERT_BLOCK, D)[dest] * gate_sorted[:, None].astype(x.dtype)
    y = jnp.zeros_like(x).at[tok_sorted].add(y_sorted)
    return y.reshape(B, T, D)


def hybrid_layer(x, sb_k_past, sb_v_past, S0, shift_prev,
                 norm1_g, w_in, rwkv_mu, rwkv_w0, rwkv_w2, rwkv_a0, rwkv_a2, rwkv_g2,
                 rwkv_k_k, rwkv_k_a, rwkv_r_k, rwkv_ln_g, rwkv_ln_b,
                 sb_q_g, sb_k_g, sb_o_g, w_out, norm2_g,
                 router_w, router_b, moe_w1, moe_b1, moe_w2, moe_b2):
    f32 = jnp.float32
    B, T, _ = x.shape
    h = rmsnorm(x, norm1_g)
    proj = h @ w_in
    u_raw, s = proj[..., :RWKV_IN], proj[..., RWKV_IN:]

    u = token_shift(u_raw, shift_prev, rwkv_mu)
    r, k, v, xw, xa, xg = jnp.split(u, RWKV_SPLITS, axis=-1)
    w_log = -jax.nn.softplus(-(rwkv_w0 + jnp.tanh(xw) @ rwkv_w2).astype(f32)) - 0.5
    decay = jnp.exp(-jnp.exp(w_log))
    a = jax.nn.sigmoid((rwkv_a0 + xa @ rwkv_a2).astype(f32))
    g = (jax.nn.sigmoid(xg) @ rwkv_g2).astype(f32)
    heads = lambda t: t.reshape(B, T, RWKV_HEADS, HEAD_DIM)
    kk = heads(k.astype(f32) * rwkv_k_k.astype(f32))
    kk = kk / jnp.maximum(jnp.sqrt(jnp.sum(kk * kk, axis=-1, keepdims=True)), 1e-12)
    kf = heads(k.astype(f32) * (1.0 + (a - 1.0) * rwkv_k_a.astype(f32)))
    rf, vf = heads(r.astype(f32)), heads(v.astype(f32))
    y, S = rwkv7_scan(S0.astype(f32), rf, heads(decay), kf, vf, kk, heads(a))
    y = head_groupnorm(y, rwkv_ln_g.reshape(RWKV_HEADS, HEAD_DIM).astype(f32),
                       rwkv_ln_b.reshape(RWKV_HEADS, HEAD_DIM).astype(f32))
    y = y + jnp.sum(rf * kf * rwkv_r_k.astype(f32), axis=-1, keepdims=True) * vf
    y_rwkv = (y.reshape(B, T, RWKV_WIDTH) * g).astype(x.dtype)

    q, kq, vq = jnp.split(s, 3, axis=-1)
    q = rmsnorm(q.reshape(B, T, SB_HEADS, HEAD_DIM), sb_q_g)
    kq = rmsnorm(kq.reshape(B, T, SB_HEADS, HEAD_DIM), sb_k_g)
    vq = vq.reshape(B, T, SB_HEADS, HEAD_DIM)
    if sb_k_past is None:
        o = sb_prompt(q, kq, vq)
    else:
        o = sb_attend(q, jnp.concatenate([sb_k_past.astype(kq.dtype), kq], axis=1),
                      jnp.concatenate([sb_v_past.astype(vq.dtype), vq], axis=1), sb_k_past.shape[1])
    o = rmsnorm(o, sb_o_g.reshape(SB_HEADS, HEAD_DIM)).reshape(B, T, SB_WIDTH)

    x = x + jnp.concatenate([y_rwkv, o.astype(x.dtype)], axis=-1) @ w_out
    x = x + moe_ffn(rmsnorm(x, norm2_g), router_w, router_b, moe_w1, moe_b1, moe_w2, moe_b2)
    return x, kq, vq, S.astype(S0.dtype), u_raw[:, -1]


def setup_inputs(seed: int = 0) -> dict:
    key = jax.random.key(seed)
    ks = iter(jax.random.split(key, 40))
    nrm = lambda shape, scale: jax.random.normal(next(ks), shape, jnp.float32) * scale
    L = DEPTH
    return {
        'x_prompt': nrm((BATCH, SEQ, D_MODEL), 1.0),
        'x_sample': nrm((DEC_BATCH, DEC_SEQ, D_MODEL), 1.0),
        'cache_sb_k': nrm((L, DEC_BATCH, PAST_LEN, SB_HEADS, HEAD_DIM), 1.0),
        'cache_sb_v': nrm((L, DEC_BATCH, PAST_LEN, SB_HEADS, HEAD_DIM), 1.0),
        'state_rwkv': nrm((L, DEC_BATCH, RWKV_HEADS, HEAD_DIM, HEAD_DIM), 0.3),
        'state_shift': nrm((L, DEC_BATCH, RWKV_IN), 1.0),
        'norm1_g': 1.0 + nrm((L, D_MODEL), 0.1),
        'w_in': nrm((L, D_MODEL, N_IN), D_MODEL ** -0.5),
        'rwkv_mu': jax.random.uniform(next(ks), (L, RWKV_IN), jnp.float32),
        'rwkv_w0': -2.0 + nrm((L, RWKV_WIDTH), 0.5),
        'rwkv_w2': nrm((L, DECAY_LORA, RWKV_WIDTH), 0.5 * DECAY_LORA ** -0.5),
        'rwkv_a0': nrm((L, RWKV_WIDTH), 0.1),
        'rwkv_a2': nrm((L, AAA_LORA, RWKV_WIDTH), 0.5 * AAA_LORA ** -0.5),
        'rwkv_g2': nrm((L, GATE_LORA, RWKV_WIDTH), GATE_LORA ** -0.5),
        'rwkv_k_k': 1.0 + nrm((L, RWKV_WIDTH), 0.1),
        'rwkv_k_a': 1.0 + nrm((L, RWKV_WIDTH), 0.1),
        'rwkv_r_k': nrm((L, RWKV_HEADS, HEAD_DIM), 0.1),
        'rwkv_ln_g': 1.0 + nrm((L, RWKV_WIDTH), 0.1),
        'rwkv_ln_b': nrm((L, RWKV_WIDTH), 0.01),
        'sb_q_g': 1.0 + nrm((L, HEAD_DIM), 0.1),
        'sb_k_g': 1.0 + nrm((L, HEAD_DIM), 0.1),
        'sb_o_g': 1.0 + nrm((L, SB_WIDTH), 0.1),
        'w_out': nrm((L, MIX_WIDTH, D_MODEL), MIX_WIDTH ** -0.5),
        'norm2_g': 1.0 + nrm((L, D_MODEL), 0.1),
        'router_w': nrm((L, D_MODEL, N_EXPERTS), D_MODEL ** -0.5),
        'router_b': nrm((L, N_EXPERTS), 0.01),
        'moe_w1': nrm((L, N_EXPERTS, D_MODEL, 2 * D_FF), D_MODEL ** -0.5),
        'moe_b1': nrm((L, N_EXPERTS, 2 * D_FF), 0.01),
        'moe_w2': nrm((L, N_EXPERTS, D_FF, D_MODEL), D_FF ** -0.5),
        'moe_b2': nrm((L, N_EXPERTS, D_MODEL), 0.01),
    }


def reference(x_prompt, x_sample, cache_sb_k, cache_sb_v, state_rwkv, state_shift,
              norm1_g, w_in, rwkv_mu, rwkv_w0, rwkv_w2, rwkv_a0, rwkv_a2, rwkv_g2,
              rwkv_k_k, rwkv_k_a, rwkv_r_k, rwkv_ln_g, rwkv_ln_b,
              sb_q_g, sb_k_g, sb_o_g, w_out, norm2_g,
              router_w, router_b, moe_w1, moe_b1, moe_w2, moe_b2):
    yp, ys = x_prompt, x_sample
    kp, vp, sp, shp = [], [], [], []
    kd, vd, sd, shd = [], [], [], []
    for l in range(DEPTH):
        w = (norm1_g[l], w_in[l], rwkv_mu[l], rwkv_w0[l], rwkv_w2[l], rwkv_a0[l], rwkv_a2[l], rwkv_g2[l],
             rwkv_k_k[l], rwkv_k_a[l], rwkv_r_k[l], rwkv_ln_g[l], rwkv_ln_b[l],
             sb_q_g[l], sb_k_g[l], sb_o_g[l], w_out[l], norm2_g[l],
             router_w[l], router_b[l], moe_w1[l], moe_b1[l], moe_w2[l], moe_b2[l])
        S0p = jnp.zeros((yp.shape[0], RWKV_HEADS, HEAD_DIM, HEAD_DIM), yp.dtype)
        sh0p = jnp.zeros((yp.shape[0], RWKV_IN), yp.dtype)
        yp, k1, v1, s1, h1 = hybrid_layer(yp, None, None, S0p, sh0p, *w)
        ys, k2, v2, s2, h2 = hybrid_layer(ys, cache_sb_k[l], cache_sb_v[l], state_rwkv[l], state_shift[l], *w)
        kp.append(k1); vp.append(v1); sp.append(s1); shp.append(h1)
        kd.append(k2); vd.append(v2); sd.append(s2); shd.append(h2)
    return (yp, ys, jnp.stack(kp), jnp.stack(vp), jnp.stack(sp), jnp.stack(shp),
            jnp.stack(kd), jnp.stack(vd), jnp.stack(sd), jnp.stack(shd))
```

```python
import functools

import jax
import jax.numpy as jnp
from jax import lax
from jax.experimental import pallas as pl
from jax.experimental.pallas import tpu as pltpu

F32 = jnp.float32
BF16 = jnp.bfloat16

LANES = 128
SUBLANES = 8
VMEM_LIMIT_BYTES = 56 * 1024 * 1024

HEAD_DIM = 64
HEADS_PER_VREG = LANES // HEAD_DIM
TOP_K = 4
RMS_EPS = 1e-5
GN_EPS = 64e-5
SWIGLU_ALPHA = 1.702
SWIGLU_LIMIT = 7.0
KK_NORM_FLOOR = 1e-12

PROJ_TILE = 512
ROW_TILE = 512
CHUNK = 64
SB_BLOCK = 128
SB_DEAD_LOGIT = -104.0
MOE_BLOCK = 512
MOE_FF_TILE = 256
GATHER_TOKENS = 64


def _cparams(sem):
    return pltpu.CompilerParams(dimension_semantics=sem, vmem_limit_bytes=VMEM_LIMIT_BYTES)


def _sigmoid(x):
    return 1.0 / (1.0 + jnp.exp(-x))


def _softplus(x):
    return jnp.maximum(x, 0.0) + jnp.log(1.0 + jnp.exp(-jnp.abs(x)))


def _split_bf16(x, parts):
    out = []
    rem = x
    for _ in range(parts):
        p = rem.astype(BF16)
        out.append(p)
        rem = rem - p.astype(F32)
    return out


def _dot(a, b):
    return jnp.dot(a, b, preferred_element_type=F32)


def _dot_nt(a, b):
    return lax.dot_general(a, b, (((1,), (1,)), ((), ())), preferred_element_type=F32)


def _dot_tn(a, b):
    return lax.dot_general(a, b, (((0,), (0,)), ((), ())), preferred_element_type=F32)


def _head_ones(n=LANES):
    r = lax.broadcasted_iota(jnp.int32, (n, n), 0) // HEAD_DIM
    c = lax.broadcasted_iota(jnp.int32, (n, n), 1) // HEAD_DIM
    return (r == c).astype(BF16)


def _head_sum(x, ones, parts=2):
    acc = None
    for p in _split_bf16(x, parts):
        t = _dot(p, ones)
        acc = t if acc is None else acc + t
    return acc


def _in_proj_kernel(x_ref, g_ref, w_ref, qg_ref, kg_ref, u_ref, q_ref, k_ref, v_ref, h_scr,
                    *, n_u, n_h):
    j = pl.program_id(1)

    @pl.when(j == 0)
    def _():
        x = x_ref[...]
        ms = jnp.mean(x * x, axis=-1, keepdims=True)
        h_scr[...] = (x * lax.rsqrt(ms + RMS_EPS) * g_ref[...]).astype(BF16)

    acc = _dot(h_scr[...], w_ref[...])

    def head_norm(y, gain):
        ones = _head_ones()
        cols = []
        for c in range(y.shape[1] // LANES):
            yc = y[:, c * LANES:(c + 1) * LANES]
            ms = _head_sum(yc * yc, ones) * (1.0 / HEAD_DIM)
            cols.append(yc * lax.rsqrt(ms + RMS_EPS))
        return jnp.concatenate(cols, axis=1) * gain

    @pl.when(j < n_u)
    def _():
        u_ref[...] = acc

    @pl.when((j >= n_u) & (j < n_u + n_h))
    def _():
        q_ref[...] = head_norm(acc, qg_ref[...])

    @pl.when((j >= n_u + n_h) & (j < n_u + 2 * n_h))
    def _():
        k_ref[...] = head_norm(acc, kg_ref[...])

    @pl.when(j >= n_u + 2 * n_h)
    def _():
        v_ref[...] = acc


def _in_proj(x2, norm_g, w_all, qg, kg, n_u, n_h):
    n, d = x2.shape
    tm = min(ROW_TILE, n)
    tn = PROJ_TILE
    n_tiles = n_u + 3 * n_h
    assert w_all.shape == (d, n_tiles * tn) and n % tm == 0
    hw = n_h * tn

    def clip_map(lo, cnt):
        return lambda i, j: (i, jnp.clip(j - lo, 0, cnt - 1))

    return pl.pallas_call(
        functools.partial(_in_proj_kernel, n_u=n_u, n_h=n_h),
        grid=(n // tm, n_tiles),
        in_specs=[
            pl.BlockSpec((tm, d), lambda i, j: (i, 0)),
            pl.BlockSpec((1, d), lambda i, j: (0, 0)),
            pl.BlockSpec((d, tn), lambda i, j: (0, j)),
            pl.BlockSpec((1, tn), lambda i, j: (0, 0)),
            pl.BlockSpec((1, tn), lambda i, j: (0, 0)),
        ],
        out_specs=[
            pl.BlockSpec((tm, tn), clip_map(0, n_u)),
            pl.BlockSpec((tm, tn), clip_map(n_u, n_h)),
            pl.BlockSpec((tm, tn), clip_map(n_u + n_h, n_h)),
            pl.BlockSpec((tm, tn), clip_map(n_u + 2 * n_h, n_h)),
        ],
        out_shape=[
            jax.ShapeDtypeStruct((n, n_u * tn), F32),
            jax.ShapeDtypeStruct((n, hw), F32),
            jax.ShapeDtypeStruct((n, hw), F32),
            jax.ShapeDtypeStruct((n, hw), F32),
        ],
        scratch_shapes=[pltpu.VMEM((tm, d), BF16)],
        compiler_params=_cparams(("parallel", "arbitrary")),
        name="in_proj",
    )(x2, norm_g, w_all, qg, kg)


def _sb_kernel(*refs, tq, n_past):
    if n_past:
        q_ref, k_ref, v_ref, pk_ref, pv_ref, og_ref, o_ref, acc_a, acc_b = refs
    else:
        q_ref, k_ref, v_ref, og_ref, o_ref, acc_a, acc_b = refs
        pk_ref = pv_ref = None
    qi = pl.program_id(2)
    lane = lax.broadcasted_iota(jnp.int32, (1, LANES), 1)
    in_a = lane < HEAD_DIM
    q = q_ref[0] * (HEAD_DIM ** -0.5)
    q_heads = (jnp.where(in_a, q, 0.0).astype(BF16), jnp.where(in_a, 0.0, q).astype(BF16))
    accs = (acc_a, acc_b)
    acc_a[...] = jnp.zeros_like(acc_a)
    acc_b[...] = jnp.zeros_like(acc_b)

    def visit(kb, vb, carries, mask):
        tk = kb.shape[0]
        s_idx = lax.broadcasted_iota(jnp.int32, (tk, tk), 0)
        j_idx = lax.broadcasted_iota(jnp.int32, (tk, tk), 1)
        suffix = (s_idx >= j_idx).astype(BF16)
        new = []
        for qh, carry, acc in zip(q_heads, carries, accs):
            z = _dot_nt(qh, kb)
            log_1mb = -_softplus(z)
            if mask is not None:
                log_1mb = jnp.where(mask, log_1mb, 0.0)
            hi, lo = _split_bf16(log_1mb, 2)
            incl = _dot(hi, suffix) + _dot(lo, suffix)
            p = jnp.exp(z + incl + carry)
            if mask is not None:
                p = jnp.where(mask, p, 0.0)
            acc[...] += _dot(p.astype(BF16), vb)
            new.append(carry + incl[:, 0:1])
        return tuple(new)

    q0 = pl.multiple_of(qi * tq, tq)
    kd = k_ref[0, pl.ds(q0, tq), :].astype(BF16)
    vd = v_ref[0, pl.ds(q0, tq), :].astype(BF16)
    t_idx = lax.broadcasted_iota(jnp.int32, (tq, tq), 0)
    j_idx = lax.broadcasted_iota(jnp.int32, (tq, tq), 1)
    zero = jnp.zeros((tq, 1), F32)
    carries = visit(kd, vd, (zero, zero), j_idx < t_idx)

    def alive(carries):
        top = jnp.max(jnp.maximum(carries[0], carries[1]))
        return (top > SB_DEAD_LOGIT).astype(jnp.int32)

    def sweep(kr, vr, n_blocks, carries):
        def cond(st):
            return (st[0] >= 0) & (st[1] > 0)

        def body(st):
            b = st[0]
            r0 = pl.multiple_of(b * SB_BLOCK, SB_BLOCK)
            kb = kr[0, pl.ds(r0, SB_BLOCK), :].astype(BF16)
            vb = vr[0, pl.ds(r0, SB_BLOCK), :].astype(BF16)
            c = visit(kb, vb, (st[2], st[3]), None)
            return (b - 1, alive(c), c[0], c[1])

        st = lax.while_loop(cond, body, (n_blocks - 1, alive(carries), carries[0], carries[1]))
        return (st[2], st[3])

    if tq == SB_BLOCK:
        carries = sweep(k_ref, v_ref, qi, carries)
    if n_past:
        carries = sweep(pk_ref, pv_ref, jnp.int32(n_past), carries)

    o = jnp.where(in_a, acc_a[...], acc_b[...])
    ms = _head_sum(o * o, _head_ones()) * (1.0 / HEAD_DIM)
    o_ref[0] = o * lax.rsqrt(ms + RMS_EPS) * og_ref[...]


def _sb_attend(q, k, v, og, past_k=None, past_v=None):
    b, t, w = q.shape
    pairs = w // LANES
    tq = min(SB_BLOCK, t)
    assert t % tq == 0 and (t == tq or tq == SB_BLOCK)
    n_past = 0
    in_specs = [
        pl.BlockSpec((1, tq, LANES), lambda bi, p, qi: (bi, qi, p)),
        pl.BlockSpec((1, t, LANES), lambda bi, p, qi: (bi, 0, p)),
        pl.BlockSpec((1, t, LANES), lambda bi, p, qi: (bi, 0, p)),
    ]
    args = [q, k, v]
    if past_k is not None:
        tp = past_k.shape[1]
        assert tp % SB_BLOCK == 0 and t == tq
        n_past = tp // SB_BLOCK
        in_specs += [pl.BlockSpec((1, tp, LANES), lambda bi, p, qi: (bi, 0, p))] * 2
        args += [past_k, past_v]
    in_specs.append(pl.BlockSpec((1, LANES), lambda bi, p, qi: (0, p)))
    args.append(og)
    return pl.pallas_call(
        functools.partial(_sb_kernel, tq=tq, n_past=n_past),
        grid=(b, pairs, t // tq),
        in_specs=in_specs,
        out_specs=pl.BlockSpec((1, tq, LANES), lambda bi, p, qi: (bi, qi, p)),
        out_shape=jax.ShapeDtypeStruct((b, t, w), F32),
        scratch_shapes=[pltpu.VMEM((tq, LANES), F32), pltpu.VMEM((tq, LANES), F32)],
        compiler_params=_cparams(("parallel", "parallel", "arbitrary")),
        name="sb_attend",
    )(*args)


def _rwkv_prep_kernel(u_ref, sh_ref, mu_ref, w0_ref, a0_ref, kk_ref, ka_ref, w2_ref, a2_ref, g2_ref,
                      r_o, lw_o, kf_o, v_o, nkk_o, b_o, g_o, prev_scr, *, width):
    ti = pl.program_id(1)
    tt = u_ref.shape[1]

    @pl.when(ti == 0)
    def _():
        prev_scr[...] = sh_ref[0]

    row = lax.broadcasted_iota(jnp.int32, (tt, 1), 0)

    def mixed(lo, hi):
        u_raw = u_ref[0, :, lo:hi]
        shifted = jnp.where(row == 0, prev_scr[:, lo:hi], pltpu.roll(u_raw, 1, axis=0))
        return u_raw + (shifted - u_raw) * mu_ref[:, lo:hi]

    r = mixed(0, width)
    k = mixed(width, 2 * width)
    v = mixed(2 * width, 3 * width)
    tail = mixed(3 * width, u_ref.shape[2])
    prev_scr[...] = u_ref[0, tt - 1:tt, :]

    lora_in = tail[:, :LANES]
    dw = _dot(jnp.tanh(lora_in).astype(BF16), w2_ref[...])
    da = _dot(lora_in.astype(BF16), a2_ref[...])
    g = _dot(_sigmoid(tail[:, LANES:]).astype(BF16), g2_ref[...])
    w_log = -_softplus(-(w0_ref[...] + dw)) - 0.5
    a = _sigmoid(a0_ref[...] + da)

    kk = k * kk_ref[...]
    ones = _head_ones()
    cols = []
    for c in range(width // LANES):
        kc = kk[:, c * LANES:(c + 1) * LANES]
        norm = jnp.sqrt(_head_sum(kc * kc, ones))
        cols.append(kc / jnp.maximum(norm, KK_NORM_FLOOR))
    kk = jnp.concatenate(cols, axis=1)

    r_o[0] = r
    lw_o[0] = -jnp.exp(w_log)
    kf_o[0] = k * (1.0 + (a - 1.0) * ka_ref[...])
    v_o[0] = v
    nkk_o[0] = -kk
    b_o[0] = kk * a
    g_o[0] = g


def _rwkv_prep(u, shift_prev, mu, w0, a0, k_k, k_a, w2p, a2p, g2p, width):
    b, t, uw = u.shape
    tt = min(256, t)
    assert t % tt == 0
    row_spec = pl.BlockSpec((1, width), lambda bi, ti: (0, 0))
    out_spec = pl.BlockSpec((1, tt, width), lambda bi, ti: (bi, ti, 0))
    return pl.pallas_call(
        functools.partial(_rwkv_prep_kernel, width=width),
        grid=(b, t // tt),
        in_specs=[
            pl.BlockSpec((1, tt, uw), lambda bi, ti: (bi, ti, 0)),
            pl.BlockSpec((1, 1, uw), lambda bi, ti: (bi, 0, 0)),
            pl.BlockSpec((1, uw), lambda bi, ti: (0, 0)),
            row_spec, row_spec, row_spec, row_spec,
            pl.BlockSpec(w2p.shape, lambda bi, ti: (0, 0)),
            pl.BlockSpec(a2p.shape, lambda bi, ti: (0, 0)),
            pl.BlockSpec(g2p.shape, lambda bi, ti: (0, 0)),
        ],
        out_specs=[out_spec] * 7,
        out_shape=[jax.ShapeDtypeStruct((b, t, width), F32)] * 7,
        scratch_shapes=[pltpu.VMEM((1, uw), F32)],
        compiler_params=_cparams(("parallel", "arbitrary")),
        name="rwkv_prep",
    )(u, shift_prev, mu, w0, a0, k_k, k_a, w2p, a2p, g2p)


def _rwkv_chunk_kernel(r_ref, lw_ref, k_ref, v_ref, nkk_ref, b_ref, g_ref, h0_ref,
                       rk_ref, lng_ref, lnb_ref, y_ref, hout_ref, h_scr, *, n_pairs, n_chunks):
    ti = pl.program_id(2)
    c_len = CHUNK

    @pl.when(ti == 0)
    def _():
        h_scr[...] = h0_ref[0]

    lane = lax.broadcasted_iota(jnp.int32, (1, LANES), 1)
    head_lanes = (lane < HEAD_DIM, lane >= HEAD_DIM)
    t_idx = lax.broadcasted_iota(jnp.int32, (c_len, c_len), 0)
    i_idx = lax.broadcasted_iota(jnp.int32, (c_len, c_len), 1)
    strict = i_idx < t_idx
    causal = i_idx <= t_idx
    prefix = causal.astype(BF16)
    eye_c = (i_idx == t_idx).astype(F32)
    r128 = lax.broadcasted_iota(jnp.int32, (LANES, LANES), 0)
    c128 = lax.broadcasted_iota(jnp.int32, (LANES, LANES), 1)
    same_head = (r128 // HEAD_DIM) == (c128 // HEAD_DIM)
    eye128 = r128 == c128
    ones = _head_ones()

    def chunk_step(ci, carry):
        r0 = pl.multiple_of(ci * c_len, c_len)
        for pp in range(n_pairs):
            cs = slice(pp * LANES, (pp + 1) * LANES)
            rows = pl.ds(r0, c_len)
            r = r_ref[0, rows, cs]
            lw = lw_ref[0, rows, cs]
            kf = k_ref[0, rows, cs]
            v = v_ref[0, rows, cs]
            nkk = nkk_ref[0, rows, cs]
            bb = b_ref[0, rows, cs]
            h = h_scr[pp]

            cum = None
            for part in _split_bf16(lw, 3):
                t = _dot(prefix, part)
                cum = t if cum is None else cum + t
            p_t = jnp.exp(cum)
            p_inv = jnp.exp(-cum)
            p_end = cum[c_len - 1:c_len, :]
            to_end = jnp.exp(p_end - cum)
            a_t = nkk * jnp.exp(cum - lw)
            b_t = bb * p_inv
            k_t = kf * p_inv
            r_t = r * p_t
            v16 = v.astype(BF16)
            b16 = b_t.astype(BF16)
            k16 = k_t.astype(BF16)

            w_heads, u_heads, q_heads, y_heads = [], [], [], []
            for sel in head_lanes:
                lhs = jnp.concatenate([jnp.where(sel, a_t, 0.0), jnp.where(sel, r_t, 0.0)], axis=0)
                lhs = lhs.astype(BF16)
                m_b = _dot_nt(lhs, b16)
                m_k = _dot_nt(lhs, k16)
                a_ab = jnp.where(strict, m_b[:c_len], 0.0)
                a_rb = jnp.where(causal, m_b[c_len:], 0.0)
                a_ak = jnp.where(strict, m_k[:c_len], 0.0)
                a_rk = jnp.where(causal, m_k[c_len:], 0.0)
                n_pow = a_ab
                t_inv = eye_c + n_pow
                span = 2
                while span < c_len:
                    n16 = n_pow.astype(BF16)
                    n_pow = _dot(n16, n16)
                    t_inv = _dot(t_inv.astype(BF16), (eye_c + n_pow).astype(BF16))
                    span *= 2
                rhs = jnp.concatenate([a_t, _dot(a_ak.astype(BF16), v16)], axis=1)
                wu = _dot(t_inv.astype(BF16), rhs.astype(BF16))
                zq = _dot(a_rb.astype(BF16), wu.astype(BF16))
                w_heads.append(wu[:, :LANES])
                u_heads.append(wu[:, LANES:])
                q_heads.append(zq[:, :LANES])
                y_heads.append(zq[:, LANES:] + _dot(a_rk.astype(BF16), v16))

            sel_a = head_lanes[0]
            w2 = jnp.where(sel_a, w_heads[0], w_heads[1])
            u0 = jnp.where(sel_a, u_heads[0], u_heads[1])
            q2 = r_t + jnp.where(sel_a, q_heads[0], q_heads[1])
            y0 = jnp.where(sel_a, y_heads[0], y_heads[1])

            h16 = h.astype(BF16)
            y = _dot(q2.astype(BF16), h16) + y0

            bk = jnp.concatenate([bb * to_end, kf * to_end], axis=0).astype(BF16)
            top = jnp.concatenate([w2, u0], axis=1)
            bot = jnp.concatenate([jnp.zeros_like(v), v], axis=1)
            gj = _dot_tn(bk, jnp.concatenate([top, bot], axis=0).astype(BF16))
            g_mat = jnp.where(eye128, jnp.exp(p_end), 0.0) + jnp.where(same_head, gj[:, :LANES], 0.0)
            j_mat = jnp.where(same_head, gj[:, LANES:], 0.0)
            h_scr[pp] = _dot(g_mat.astype(BF16), h16) + j_mat

            mean = _head_sum(y, ones) * (1.0 / HEAD_DIM)
            yc = y - mean
            var = _head_sum(yc * yc, ones) * (1.0 / HEAD_DIM)
            yn = yc * lax.rsqrt(var + GN_EPS) * lng_ref[:, cs] + lnb_ref[:, cs]
            bonus = _head_sum(r * kf * rk_ref[:, cs], ones)
            y_ref[0, rows, cs] = (yn + bonus * v) * g_ref[0, rows, cs]
        return carry

    lax.fori_loop(0, n_chunks, chunk_step, 0)

    @pl.when(ti == pl.num_programs(2) - 1)
    def _():
        hout_ref[0] = h_scr[...]


def _rwkv_chunk(r, lw, kf, v, nkk, bb, g, h0, r_k, ln_g, ln_b):
    b, t, w = r.shape
    pairs = w // LANES
    tt = min(512, t)
    assert t % tt == 0 and tt % CHUNK == 0
    n_pairs = max(1, min(pairs, 512 // tt))
    assert pairs % n_pairs == 0
    cw = n_pairs * LANES
    seq = pl.BlockSpec((1, tt, cw), lambda bi, p, ti: (bi, ti, p))
    par = pl.BlockSpec((1, cw), lambda bi, p, ti: (0, p))
    st = pl.BlockSpec((1, n_pairs, LANES, LANES), lambda bi, p, ti: (bi, p, 0, 0))
    return pl.pallas_call(
        functools.partial(_rwkv_chunk_kernel, n_pairs=n_pairs, n_chunks=tt // CHUNK),
        grid=(b, pairs // n_pairs, t // tt),
        in_specs=[seq] * 7 + [st, par, par, par],
        out_specs=[seq, st],
        out_shape=[jax.ShapeDtypeStruct((b, t, w), F32),
                   jax.ShapeDtypeStruct((b, pairs, LANES, LANES), F32)],
        scratch_shapes=[pltpu.VMEM((n_pairs, LANES, LANES), F32)],
        compiler_params=_cparams(("parallel", "parallel", "arbitrary")),
        name="rwkv_chunk",
    )(r, lw, kf, v, nkk, bb, g, h0, r_k, ln_g, ln_b)


def _state_to_blockdiag(s):
    b, h, dv, dk = s.shape
    st = jnp.swapaxes(s, -1, -2).reshape(b, h // 2, 2, dk, dv)
    z = jnp.zeros_like(st[:, :, 0])
    top = jnp.concatenate([st[:, :, 0], z], axis=-1)
    bot = jnp.concatenate([z, st[:, :, 1]], axis=-1)
    return jnp.concatenate([top, bot], axis=-2)


def _blockdiag_to_state(hm):
    b, p, _, _ = hm.shape
    a = hm[:, :, :HEAD_DIM, :HEAD_DIM]
    c = hm[:, :, HEAD_DIM:, HEAD_DIM:]
    st = jnp.stack([a, c], axis=2).reshape(b, 2 * p, HEAD_DIM, HEAD_DIM)
    return jnp.swapaxes(st, -1, -2)


def _out_proj_kernel(x_ref, yr_ref, o_ref, w_ref, g_ref, rw_ref, rb_ref,
                     x1_ref, h2_ref, e_ref, gate_ref):
    half = yr_ref.shape[1]
    x1 = (x_ref[...] + _dot(yr_ref[...].astype(BF16), w_ref[:half, :])
          + _dot(o_ref[...].astype(BF16), w_ref[half:, :]))
    x1_ref[...] = x1
    ms = jnp.mean(x1 * x1, axis=-1, keepdims=True)
    h2 = x1 * lax.rsqrt(ms + RMS_EPS) * g_ref[...]
    h2_ref[...] = h2
    logits = jnp.dot(h2, rw_ref[...], preferred_element_type=F32,
                     precision=lax.Precision.HIGHEST) + rb_ref[...]
    lane = lax.broadcasted_iota(jnp.int32, logits.shape, 1)
    lane_f = lane.astype(F32)
    experts = jnp.zeros(logits.shape, F32)
    tops = jnp.zeros(logits.shape, F32)
    work = logits
    top0 = None
    for kth in range(TOP_K):
        m = jnp.max(work, axis=-1, keepdims=True)
        idx = jnp.min(jnp.where(work == m, lane_f, float(LANES)), axis=-1, keepdims=True)
        if top0 is None:
            top0 = m
        experts = jnp.where(lane == kth, idx, experts)
        tops = jnp.where(lane == kth, jnp.exp(m - top0), tops)
        work = jnp.where(lane_f == idx, -jnp.inf, work)
    e_ref[...] = experts.astype(jnp.int32)
    gate_ref[...] = tops / jnp.sum(tops, axis=-1, keepdims=True)


def _out_proj(x2, yr, o, w_out, norm_g, rw, rb):
    n, d = x2.shape
    tm = min(256, n)
    assert n % tm == 0
    half = yr.shape[1]
    row = lambda i: (i, 0)
    fixed = lambda i: (0, 0)
    return pl.pallas_call(
        _out_proj_kernel,
        grid=(n // tm,),
        in_specs=[
            pl.BlockSpec((tm, d), row),
            pl.BlockSpec((tm, half), row),
            pl.BlockSpec((tm, half), row),
            pl.BlockSpec(w_out.shape, fixed),
            pl.BlockSpec((1, d), fixed),
            pl.BlockSpec(rw.shape, fixed),
            pl.BlockSpec((1, LANES), fixed),
        ],
        out_specs=[
            pl.BlockSpec((tm, d), row),
            pl.BlockSpec((tm, d), row),
            pl.BlockSpec((tm, LANES), row),
            pl.BlockSpec((tm, LANES), row),
        ],
        out_shape=[
            jax.ShapeDtypeStruct((n, d), F32),
            jax.ShapeDtypeStruct((n, d), F32),
            jax.ShapeDtypeStruct((n, LANES), jnp.int32),
            jax.ShapeDtypeStruct((n, LANES), F32),
        ],
        compiler_params=_cparams(("parallel",)),
        name="out_proj",
    )(x2, yr, o, w_out, norm_g, rw, rb)


def _dispatch_kernel(nv_ref, dest_ref, h2_hbm, xbuf_hbm, zero_scr, sem, *, rows, n_blocks):
    @pl.when(pl.program_id(0) == 0)
    def _():
        zero_scr[...] = jnp.zeros_like(zero_scr)

        def fill(i):
            r0 = pl.multiple_of(i * MOE_BLOCK, MOE_BLOCK)
            return pltpu.make_async_copy(zero_scr, xbuf_hbm.at[pl.ds(r0, MOE_BLOCK)], sem)

        def fill_start(i, c):
            @pl.when(nv_ref[i] < MOE_BLOCK)
            def _():
                fill(i).start()
            return c

        def fill_wait(i, c):
            @pl.when(nv_ref[i] < MOE_BLOCK)
            def _():
                fill(i).wait()
            return c

        lax.fori_loop(0, n_blocks, fill_start, 0)
        lax.fori_loop(0, n_blocks, fill_wait, 0)

    base_tok = pl.program_id(0) * (rows // TOP_K)

    def copy(a):
        return pltpu.make_async_copy(h2_hbm.at[pl.ds(base_tok + a // TOP_K, 1)],
                                     xbuf_hbm.at[pl.ds(dest_ref[0, 0, a], 1)], sem)

    def start(a, c):
        copy(a).start()
        return c

    def wait(a, c):
        copy(a).wait()
        return c

    lax.fori_loop(0, rows, start, 0)
    lax.fori_loop(0, rows, wait, 0)


def _dispatch(h2, dest, n_valid):
    n, d = h2.shape
    rows = 512
    n_asg = n * TOP_K
    assert n_asg % rows == 0
    steps = n_asg // rows
    n_blocks = n_valid.shape[0]
    grid_spec = pltpu.PrefetchScalarGridSpec(
        num_scalar_prefetch=1,
        grid=(steps,),
        in_specs=[
            pl.BlockSpec((1, 1, rows), lambda i, nv: (i, 0, 0), memory_space=pltpu.SMEM),
            pl.BlockSpec(memory_space=pl.ANY),
        ],
        out_specs=pl.BlockSpec(memory_space=pl.ANY),
        scratch_shapes=[pltpu.VMEM((MOE_BLOCK, d), F32), pltpu.SemaphoreType.DMA],
    )
    return pl.pallas_call(
        functools.partial(_dispatch_kernel, rows=rows, n_blocks=n_blocks),
        grid_spec=grid_spec,
        out_shape=jax.ShapeDtypeStruct((n_blocks * MOE_BLOCK, d), F32),
        compiler_params=_cparams(("arbitrary",)),
        name="moe_dispatch",
    )(n_valid, dest.reshape(steps, 1, rows), h2)


def _moe_ffn_kernel(be_ref, nv_ref, x_ref, w1g_ref, w1l_ref, b1g_ref, b1l_ref, w2_ref, b2_ref,
                    y_ref, xs_scr):
    i = pl.program_id(0)
    f = pl.program_id(1)
    n_valid = nv_ref[i]

    @pl.when(n_valid == 0)
    def _():
        y_ref[...] = jnp.zeros_like(y_ref)

    @pl.when(n_valid > 0)
    def _():
        @pl.when(f == 0)
        def _():
            xs_scr[...] = x_ref[...].astype(BF16)

        xs = xs_scr[...]
        hg = _dot(xs, w1g_ref[0].astype(BF16)) + b1g_ref[0]
        hl = _dot(xs, w1l_ref[0].astype(BF16)) + b1l_ref[0]
        glu = jnp.minimum(hg, SWIGLU_LIMIT)
        lin = jnp.clip(hl, -SWIGLU_LIMIT, SWIGLU_LIMIT)
        act = glu * _sigmoid(SWIGLU_ALPHA * glu) * (lin + 1.0)
        part = _dot(act.astype(BF16), w2_ref[0].astype(BF16))

        @pl.when(f == 0)
        def _():
            y_ref[...] = part + b2_ref[0]

        @pl.when(f > 0)
        def _():
            y_ref[...] += part


def _moe_ffn(xbuf, block_e, n_valid, w1, b1, w2, b2):
    n_rows, d = xbuf.shape
    n_exp, _, ff2 = w1.shape
    ff = ff2 // 2
    bm, tf = MOE_BLOCK, MOE_FF_TILE
    n_f = ff // tf
    n_blocks = n_rows // bm
    b1r = b1.reshape(n_exp, 1, ff2)
    b2r = b2.reshape(n_exp, 1, d)
    grid_spec = pltpu.PrefetchScalarGridSpec(
        num_scalar_prefetch=2,
        grid=(n_blocks, n_f),
        in_specs=[
            pl.BlockSpec((bm, d), lambda i, f, be, nv: (i, 0)),
            pl.BlockSpec((1, d, tf), lambda i, f, be, nv: (be[i], 0, f)),
            pl.BlockSpec((1, d, tf), lambda i, f, be, nv: (be[i], 0, n_f + f)),
            pl.BlockSpec((1, 1, tf), lambda i, f, be, nv: (be[i], 0, f)),
            pl.BlockSpec((1, 1, tf), lambda i, f, be, nv: (be[i], 0, n_f + f)),
            pl.BlockSpec((1, tf, d), lambda i, f, be, nv: (be[i], f, 0)),
            pl.BlockSpec((1, 1, d), lambda i, f, be, nv: (be[i], 0, 0)),
        ],
        out_specs=pl.BlockSpec((bm, d), lambda i, f, be, nv: (i, 0)),
        scratch_shapes=[pltpu.VMEM((bm, d), BF16)],
    )
    return pl.pallas_call(
        _moe_ffn_kernel,
        grid_spec=grid_spec,
        out_shape=jax.ShapeDtypeStruct((n_rows, d), F32),
        compiler_params=_cparams(("parallel", "arbitrary")),
        name="moe_ffn",
    )(block_e, n_valid, xbuf, w1, w1, b1r, b1r, w2, b2r)


def _combine_kernel(dest_ref, x1_ref, gate_ref, ybuf_hbm, out_ref, rows_scr, sem, *, tokens):
    def copy(a):
        return pltpu.make_async_copy(ybuf_hbm.at[pl.ds(dest_ref[0, 0, a], 1)],
                                     rows_scr.at[a % TOP_K, pl.ds(a // TOP_K, 1)], sem)

    def start(a, c):
        copy(a).start()
        return c

    def wait(a, c):
        copy(a).wait()
        return c

    lax.fori_loop(0, tokens * TOP_K, start, 0)
    lax.fori_loop(0, tokens * TOP_K, wait, 0)
    acc = x1_ref[...]
    gate = gate_ref[...]
    for kth in range(TOP_K):
        acc = acc + gate[:, kth:kth + 1] * rows_scr[kth]
    out_ref[...] = acc


def _combine(x1, gate, dest, ybuf):
    n, d = x1.shape
    tokens = min(GATHER_TOKENS, n)
    assert n % tokens == 0
    steps = n // tokens
    return pl.pallas_call(
        functools.partial(_combine_kernel, tokens=tokens),
        grid=(steps,),
        in_specs=[
            pl.BlockSpec((1, 1, tokens * TOP_K), lambda i: (i, 0, 0), memory_space=pltpu.SMEM),
            pl.BlockSpec((tokens, d), lambda i: (i, 0)),
            pl.BlockSpec((tokens, LANES), lambda i: (i, 0)),
            pl.BlockSpec(memory_space=pl.ANY),
        ],
        out_specs=pl.BlockSpec((tokens, d), lambda i: (i, 0)),
        out_shape=jax.ShapeDtypeStruct((n, d), F32),
        scratch_shapes=[pltpu.VMEM((TOP_K, tokens, d), F32), pltpu.SemaphoreType.DMA],
        compiler_params=_cparams(("arbitrary",)),
        name="moe_combine",
    )(dest.reshape(steps, 1, tokens * TOP_K), x1, gate, ybuf)


def _route(experts, n_exp, n_blocks):
    flat_e = experts.reshape(-1)
    onehot = (flat_e[:, None] == jnp.arange(n_exp, dtype=jnp.int32)[None, :]).astype(jnp.int32)
    csum = jnp.cumsum(onehot, axis=0)
    counts = csum[-1]
    padded = (counts + MOE_BLOCK - 1) // MOE_BLOCK * MOE_BLOCK
    pend = jnp.cumsum(padded)
    pstart = pend - padded
    dest = jnp.sum(onehot * (csum - 1 + pstart[None, :]), axis=1)
    block_start = jnp.arange(n_blocks, dtype=jnp.int32) * MOE_BLOCK
    block_e = jnp.minimum(jnp.sum(block_start[:, None] >= pend[None, :], axis=1), n_exp - 1)
    n_valid = jnp.clip(counts[block_e] - (block_start - pstart[block_e]), 0, MOE_BLOCK)
    return dest.astype(jnp.int32), block_e.astype(jnp.int32), n_valid.astype(jnp.int32)


def _mixers(x, past_k, past_v, s0, shift_prev, wts):
    b, t, d = x.shape
    width = wts["width"]
    u, q, k, v = _in_proj(x.reshape(b * t, d), wts["norm1_g"], wts["w_all"], wts["qg"], wts["kg"],
                          wts["n_u"], wts["n_h"])
    uw = u.shape[1]
    u = u.reshape(b, t, uw)
    q, k, v = (a.reshape(b, t, width) for a in (q, k, v))
    o = _sb_attend(q, k, v, wts["sb_o_g"], past_k, past_v)

    shift_p = jnp.pad(shift_prev, ((0, 0), (0, uw - shift_prev.shape[1]))).reshape(b, 1, uw)
    r, lw, kf, vv, nkk, bb, g = _rwkv_prep(u, shift_p, wts["mu"], wts["w0"], wts["a0"], wts["k_k"],
                                           wts["k_a"], wts["w2p"], wts["a2p"], wts["g2p"], width)
    y_rwkv, h_fin = _rwkv_chunk(r, lw, kf, vv, nkk, bb, g, _state_to_blockdiag(s0),
                                wts["r_k"], wts["ln_g"], wts["ln_b"])
    x1, h2, experts, gates = _out_proj(x.reshape(b * t, d), y_rwkv.reshape(b * t, width),
                                       o.reshape(b * t, width), wts["w_out"], wts["norm2_g"],
                                       wts["rw"], wts["rb"])
    new_k = k.reshape(b, t, width // HEAD_DIM, HEAD_DIM)
    new_v = v.reshape(b, t, width // HEAD_DIM, HEAD_DIM)
    new_shift = u[:, -1, :wts["rwkv_in"]]
    return x1, h2, experts[:, :TOP_K], gates, new_k, new_v, _blockdiag_to_state(h_fin), new_shift


def kernel(x_prompt, x_sample, cache_sb_k, cache_sb_v, state_rwkv, state_shift, norm1_g, w_in, rwkv_mu, rwkv_w0, rwkv_w2, rwkv_a0, rwkv_a2, rwkv_g2, rwkv_k_k, rwkv_k_a, rwkv_r_k, rwkv_ln_g, rwkv_ln_b, sb_q_g, sb_k_g, sb_o_g, w_out, norm2_g, router_w, router_b, moe_w1, moe_b1, moe_w2, moe_b2):
    depth = w_in.shape[0]
    assert depth == 1
    layer = 0
    d = x_prompt.shape[-1]
    width = rwkv_w0.shape[-1]
    rwkv_in = rwkv_mu.shape[-1]
    n_exp = router_w.shape[-1]
    decay_lora, aaa_lora, gate_lora = rwkv_w2.shape[1], rwkv_a2.shape[1], rwkv_g2.shape[1]
    assert decay_lora + aaa_lora == LANES and width % PROJ_TILE == 0
    n_h = width // PROJ_TILE
    n_u = -(-rwkv_in // PROJ_TILE)
    uw = n_u * PROJ_TILE
    tail_w = uw - 3 * width

    w_l = w_in[layer]
    w_all = jnp.concatenate(
        [w_l[:, :rwkv_in], jnp.zeros((d, uw - rwkv_in), F32), w_l[:, rwkv_in:]], axis=1).astype(BF16)
    reps = PROJ_TILE // HEAD_DIM
    row = lambda a: a.reshape(1, -1).astype(F32)
    wts = dict(
        width=width, rwkv_in=rwkv_in, n_u=n_u, n_h=n_h,
        norm1_g=row(norm1_g[layer]), w_all=w_all,
        qg=row(jnp.tile(sb_q_g[layer], reps)), kg=row(jnp.tile(sb_k_g[layer], reps)),
        sb_o_g=row(sb_o_g[layer]),
        mu=jnp.pad(row(rwkv_mu[layer]), ((0, 0), (0, uw - rwkv_in))),
        w0=row(rwkv_w0[layer]), a0=row(rwkv_a0[layer]),
        k_k=row(rwkv_k_k[layer]), k_a=row(rwkv_k_a[layer]),
        w2p=jnp.pad(rwkv_w2[layer], ((0, aaa_lora), (0, 0))).astype(BF16),
        a2p=jnp.pad(rwkv_a2[layer], ((decay_lora, 0), (0, 0))).astype(BF16),
        g2p=jnp.pad(rwkv_g2[layer], ((0, tail_w - LANES - gate_lora), (0, 0))).astype(BF16),
        r_k=row(rwkv_r_k[layer]), ln_g=row(rwkv_ln_g[layer]), ln_b=row(rwkv_ln_b[layer]),
        w_out=w_out[layer].astype(BF16), norm2_g=row(norm2_g[layer]),
        rw=jnp.pad(router_w[layer], ((0, 0), (0, LANES - n_exp))),
        rb=jnp.pad(row(router_b[layer]), ((0, 0), (0, LANES - n_exp)), constant_values=-jnp.inf),
    )

    bp, tp, _ = x_prompt.shape
    bs, ts, _ = x_sample.shape
    heads = width // HEAD_DIM
    s0_prompt = jnp.zeros((bp, heads, HEAD_DIM, HEAD_DIM), F32)
    shift0_prompt = jnp.zeros((bp, rwkv_in), F32)
    past_k = cache_sb_k[layer].reshape(bs, -1, width)
    past_v = cache_sb_v[layer].reshape(bs, -1, width)

    grp_p = _mixers(x_prompt, None, None, s0_prompt, shift0_prompt, wts)
    grp_s = _mixers(x_sample, past_k, past_v, state_rwkv[layer], state_shift[layer], wts)

    n_p, n_s = bp * tp, bs * ts
    n_tok = n_p + n_s
    h2 = jnp.concatenate([grp_p[1], grp_s[1]], axis=0)
    experts = jnp.concatenate([grp_p[2], grp_s[2]], axis=0)
    n_blocks = -(-(n_tok * TOP_K) // MOE_BLOCK) + n_exp
    dest, block_e, n_valid = _route(experts, n_exp, n_blocks)
    xbuf = _dispatch(h2, dest, n_valid)
    ybuf = _moe_ffn(xbuf, block_e, n_valid, moe_w1[layer], moe_b1[layer], moe_w2[layer], moe_b2[layer])
    y_p = _combine(grp_p[0], grp_p[3], dest[:n_p * TOP_K], ybuf).reshape(bp, tp, d)
    y_s = _combine(grp_s[0], grp_s[3], dest[n_p * TOP_K:], ybuf).reshape(bs, ts, d)

    lead = lambda a: a[None]
    return (y_p, y_s,
            lead(grp_p[4]), lead(grp_p[5]), lead(grp_p[6]), lead(grp_p[7]),
            lead(grp_s[4]), lead(grp_s[5]), lead(grp_s[6]), lead(grp_s[7]))
```

```python
import functools

import jax
import jax.numpy as jnp
from jax import lax
from jax.experimental import pallas as pl
from jax.experimental.pallas import tpu as pltpu

F32 = jnp.float32
BF16 = jnp.bfloat16

LANES = 128
SUBLANES = 8
VMEM_LIMIT_BYTES = 56 * 1024 * 1024

HEAD_DIM = 64
HEADS_PER_VREG = LANES // HEAD_DIM
TOP_K = 4
RMS_EPS = 1e-5
GN_EPS = 64e-5
SWIGLU_ALPHA = 1.702
SWIGLU_LIMIT = 7.0
KK_NORM_FLOOR = 1e-12

PROJ_TILE = 512
ROW_TILE = 512
CHUNK = 64
SB_BLOCK = 128
SB_DEAD_LOGIT = -104.0
MOE_BLOCK = 512
MOE_FF_TILE = 256
GATHER_TOKENS = 64


def _cparams(sem):
    return pltpu.CompilerParams(dimension_semantics=sem, vmem_limit_bytes=VMEM_LIMIT_BYTES)


def _sigmoid(x):
    return 1.0 / (1.0 + jnp.exp(-x))


def _softplus(x):
    return jnp.maximum(x, 0.0) + jnp.log(1.0 + jnp.exp(-jnp.abs(x)))


def _split_bf16(x, parts):
    out = []
    rem = x
    for _ in range(parts):
        p = rem.astype(BF16)
        out.append(p)
        rem = rem - p.astype(F32)
    return out


def _dot(a, b):
    return jnp.dot(a, b, preferred_element_type=F32)


def _dot_nt(a, b):
    return lax.dot_general(a, b, (((1,), (1,)), ((), ())), preferred_element_type=F32)


def _dot_tn(a, b):
    return lax.dot_general(a, b, (((0,), (0,)), ((), ())), preferred_element_type=F32)


def _head_ones(n=LANES):
    r = lax.broadcasted_iota(jnp.int32, (n, n), 0) // HEAD_DIM
    c = lax.broadcasted_iota(jnp.int32, (n, n), 1) // HEAD_DIM
    return (r == c).astype(BF16)


def _head_sum(x, ones, parts=2):
    acc = None
    for p in _split_bf16(x, parts):
        t = _dot(p, ones)
        acc = t if acc is None else acc + t
    return acc


def _in_proj_kernel(x_ref, g_ref, w_ref, qg_ref, kg_ref, u_ref, q_ref, k_ref, v_ref, h_scr,
                    *, n_u, n_h):
    j = pl.program_id(1)

    @pl.when(j == 0)
    def _():
        x = x_ref[...]
        ms = jnp.mean(x * x, axis=-1, keepdims=True)
        h_scr[...] = (x * lax.rsqrt(ms + RMS_EPS) * g_ref[...]).astype(BF16)

    acc = _dot(h_scr[...], w_ref[...])

    def head_norm(y, gain):
        ones = _head_ones()
        cols = []
        for c in range(y.shape[1] // LANES):
            yc = y[:, c * LANES:(c + 1) * LANES]
            ms = _head_sum(yc * yc, ones) * (1.0 / HEAD_DIM)
            cols.append(yc * lax.rsqrt(ms + RMS_EPS))
        return jnp.concatenate(cols, axis=1) * gain

    @pl.when(j < n_u)
    def _():
        u_ref[...] = acc

    @pl.when((j >= n_u) & (j < n_u + n_h))
    def _():
        q_ref[...] = head_norm(acc, qg_ref[...])

    @pl.when((j >= n_u + n_h) & (j < n_u + 2 * n_h))
    def _():
        k_ref[...] = head_norm(acc, kg_ref[...])

    @pl.when(j >= n_u + 2 * n_h)
    def _():
        v_ref[...] = acc


def _in_proj(x2, norm_g, w_all, qg, kg, n_u, n_h):
    n, d = x2.shape
    tm = min(ROW_TILE, n)
    tn = PROJ_TILE
    n_tiles = n_u + 3 * n_h
    assert w_all.shape == (d, n_tiles * tn) and n % tm == 0
    hw = n_h * tn

    def clip_map(lo, cnt):
        return lambda i, j: (i, jnp.clip(j - lo, 0, cnt - 1))

    return pl.pallas_call(
        functools.partial(_in_proj_kernel, n_u=n_u, n_h=n_h),
        grid=(n // tm, n_tiles),
        in_specs=[
            pl.BlockSpec((tm, d), lambda i, j: (i, 0)),
            pl.BlockSpec((1, d), lambda i, j: (0, 0)),
            pl.BlockSpec((d, tn), lambda i, j: (0, j)),
            pl.BlockSpec((1, tn), lambda i, j: (0, 0)),
            pl.BlockSpec((1, tn), lambda i, j: (0, 0)),
        ],
        out_specs=[
            pl.BlockSpec((tm, tn), clip_map(0, n_u)),
            pl.BlockSpec((tm, tn), clip_map(n_u, n_h)),
            pl.BlockSpec((tm, tn), clip_map(n_u + n_h, n_h)),
            pl.BlockSpec((tm, tn), clip_map(n_u + 2 * n_h, n_h)),
        ],
        out_shape=[
            jax.ShapeDtypeStruct((n, n_u * tn), F32),
            jax.ShapeDtypeStruct((n, hw), F32),
            jax.ShapeDtypeStruct((n, hw), F32),
            jax.ShapeDtypeStruct((n, hw), F32),
        ],
        scratch_shapes=[pltpu.VMEM((tm, d), BF16)],
        compiler_params=_cparams(("parallel", "arbitrary")),
        name="in_proj",
    )(x2, norm_g, w_all, qg, kg)


def _sb_kernel(*refs, tq, n_past, n_grp):
    if n_past:
        q_ref, k_ref, v_ref, pk_ref, pv_ref, og_ref, o_ref, acc_ref = refs
    else:
        q_ref, k_ref, v_ref, og_ref, o_ref, acc_ref = refs
        pk_ref = pv_ref = None
    qi = pl.program_id(2)
    n_heads = HEADS_PER_VREG * n_grp
    lane = lax.broadcasted_iota(jnp.int32, (1, LANES), 1)
    in_a = lane < HEAD_DIM
    q_heads = []
    for g in range(n_grp):
        q = q_ref[0, :, g * LANES:(g + 1) * LANES] * (HEAD_DIM ** -0.5)
        q_heads.append(jnp.where(in_a, q, 0.0).astype(BF16))
        q_heads.append(jnp.where(in_a, 0.0, q).astype(BF16))
    acc_ref[...] = jnp.zeros_like(acc_ref)

    def visit(k_src, v_src, r0, tk, carries, mask):
        rows = pl.ds(r0, tk)
        s_idx = lax.broadcasted_iota(jnp.int32, (tk, tk), 0)
        j_idx = lax.broadcasted_iota(jnp.int32, (tk, tk), 1)
        suffix = (s_idx >= j_idx).astype(BF16)
        kbs = [k_src[0, rows, g * LANES:(g + 1) * LANES].astype(BF16) for g in range(n_grp)]
        zs = [_dot_nt(q_heads[h], kbs[h // HEADS_PER_VREG]) for h in range(n_heads)]
        splits = []
        for z in zs:
            log_1mb = -_softplus(z)
            if mask is not None:
                log_1mb = jnp.where(mask, log_1mb, 0.0)
            splits.append(_split_bf16(log_1mb, 2))
        incls = [_dot(hi, suffix) + _dot(lo, suffix) for hi, lo in splits]
        ps = []
        for z, incl, carry in zip(zs, incls, carries):
            p = jnp.exp(z + incl + carry)
            if mask is not None:
                p = jnp.where(mask, p, 0.0)
            ps.append(p.astype(BF16))
        vbs = [v_src[0, rows, g * LANES:(g + 1) * LANES].astype(BF16) for g in range(n_grp)]
        for h in range(n_heads):
            acc_ref[h] += _dot(ps[h], vbs[h // HEADS_PER_VREG])
        return tuple(c + incl[:, 0:1] for c, incl in zip(carries, incls))

    q0 = pl.multiple_of(qi * tq, tq)
    t_idx = lax.broadcasted_iota(jnp.int32, (tq, tq), 0)
    j_idx = lax.broadcasted_iota(jnp.int32, (tq, tq), 1)
    zero = jnp.zeros((tq, 1), F32)
    carries = visit(k_ref, v_ref, q0, tq, (zero,) * n_heads, j_idx < t_idx)

    def alive(carries):
        top = carries[0]
        for c in carries[1:]:
            top = jnp.maximum(top, c)
        return (jnp.max(top) > SB_DEAD_LOGIT).astype(jnp.int32)

    def sweep(kr, vr, n_blocks, carries):
        def cond(st):
            return (st[0] >= 0) & (st[1] > 0)

        def body(st):
            b = st[0]
            r0 = pl.multiple_of(b * SB_BLOCK, SB_BLOCK)
            c = visit(kr, vr, r0, SB_BLOCK, st[2:], None)
            return (b - 1, alive(c)) + c

        st = lax.while_loop(cond, body, (n_blocks - 1, alive(carries)) + carries)
        return st[2:]

    if tq == SB_BLOCK:
        carries = sweep(k_ref, v_ref, qi, carries)
    if n_past:
        carries = sweep(pk_ref, pv_ref, jnp.int32(n_past), carries)

    ones = _head_ones()
    for g in range(n_grp):
        o = jnp.where(in_a, acc_ref[HEADS_PER_VREG * g], acc_ref[HEADS_PER_VREG * g + 1])
        ms = _head_sum(o * o, ones) * (1.0 / HEAD_DIM)
        cs = slice(g * LANES, (g + 1) * LANES)
        o_ref[0, :, cs] = o * lax.rsqrt(ms + RMS_EPS) * og_ref[:, cs]


def _sb_attend(q, k, v, og, past_k=None, past_v=None):
    b, t, w = q.shape
    n_grp = 2
    gw = n_grp * LANES
    assert w % gw == 0
    tq = min(SB_BLOCK, t)
    assert t % tq == 0 and (t == tq or tq == SB_BLOCK)
    n_past = 0
    in_specs = [
        pl.BlockSpec((1, tq, gw), lambda bi, p, qi: (bi, qi, p)),
        pl.BlockSpec((1, t, gw), lambda bi, p, qi: (bi, 0, p)),
        pl.BlockSpec((1, t, gw), lambda bi, p, qi: (bi, 0, p)),
    ]
    args = [q, k, v]
    if past_k is not None:
        tp = past_k.shape[1]
        assert tp % SB_BLOCK == 0 and t == tq
        n_past = tp // SB_BLOCK
        in_specs += [pl.BlockSpec((1, tp, gw), lambda bi, p, qi: (bi, 0, p))] * 2
        args += [past_k, past_v]
    in_specs.append(pl.BlockSpec((1, gw), lambda bi, p, qi: (0, p)))
    args.append(og)
    return pl.pallas_call(
        functools.partial(_sb_kernel, tq=tq, n_past=n_past, n_grp=n_grp),
        grid=(b, w // gw, t // tq),
        in_specs=in_specs,
        out_specs=pl.BlockSpec((1, tq, gw), lambda bi, p, qi: (bi, qi, p)),
        out_shape=jax.ShapeDtypeStruct((b, t, w), F32),
        scratch_shapes=[pltpu.VMEM((HEADS_PER_VREG * n_grp, tq, LANES), F32)],
        compiler_params=_cparams(("parallel", "parallel", "arbitrary")),
        name="sb_attend",
    )(*args)


def _rwkv_prep_kernel(u_ref, sh_ref, mu_ref, w0_ref, a0_ref, kk_ref, ka_ref, w2_ref, a2_ref, g2_ref,
                      r_o, lw_o, kf_o, v_o, nkk_o, b_o, g_o, prev_scr, *, width):
    ti = pl.program_id(1)
    tt = u_ref.shape[1]

    @pl.when(ti == 0)
    def _():
        prev_scr[...] = sh_ref[0]

    row = lax.broadcasted_iota(jnp.int32, (tt, 1), 0)

    def mixed(lo, hi):
        u_raw = u_ref[0, :, lo:hi]
        shifted = jnp.where(row == 0, prev_scr[:, lo:hi], pltpu.roll(u_raw, 1, axis=0))
        return u_raw + (shifted - u_raw) * mu_ref[:, lo:hi]

    r = mixed(0, width)
    k = mixed(width, 2 * width)
    v = mixed(2 * width, 3 * width)
    tail = mixed(3 * width, u_ref.shape[2])
    prev_scr[...] = u_ref[0, tt - 1:tt, :]

    lora_in = tail[:, :LANES]
    dw = _dot(jnp.tanh(lora_in).astype(BF16), w2_ref[...])
    da = _dot(lora_in.astype(BF16), a2_ref[...])
    g = _dot(_sigmoid(tail[:, LANES:]).astype(BF16), g2_ref[...])
    w_log = -_softplus(-(w0_ref[...] + dw)) - 0.5
    a = _sigmoid(a0_ref[...] + da)

    kk = k * kk_ref[...]
    ones = _head_ones()
    cols = []
    for c in range(width // LANES):
        kc = kk[:, c * LANES:(c + 1) * LANES]
        norm = jnp.sqrt(_head_sum(kc * kc, ones))
        cols.append(kc / jnp.maximum(norm, KK_NORM_FLOOR))
    kk = jnp.concatenate(cols, axis=1)

    r_o[0] = r
    lw_o[0] = -jnp.exp(w_log)
    kf_o[0] = k * (1.0 + (a - 1.0) * ka_ref[...])
    v_o[0] = v
    nkk_o[0] = -kk
    b_o[0] = kk * a
    g_o[0] = g


def _rwkv_prep(u, shift_prev, mu, w0, a0, k_k, k_a, w2p, a2p, g2p, width):
    b, t, uw = u.shape
    tt = min(256, t)
    assert t % tt == 0
    row_spec = pl.BlockSpec((1, width), lambda bi, ti: (0, 0))
    out_spec = pl.BlockSpec((1, tt, width), lambda bi, ti: (bi, ti, 0))
    return pl.pallas_call(
        functools.partial(_rwkv_prep_kernel, width=width),
        grid=(b, t // tt),
        in_specs=[
            pl.BlockSpec((1, tt, uw), lambda bi, ti: (bi, ti, 0)),
            pl.BlockSpec((1, 1, uw), lambda bi, ti: (bi, 0, 0)),
            pl.BlockSpec((1, uw), lambda bi, ti: (0, 0)),
            row_spec, row_spec, row_spec, row_spec,
            pl.BlockSpec(w2p.shape, lambda bi, ti: (0, 0)),
            pl.BlockSpec(a2p.shape, lambda bi, ti: (0, 0)),
            pl.BlockSpec(g2p.shape, lambda bi, ti: (0, 0)),
        ],
        out_specs=[out_spec] * 7,
        out_shape=[jax.ShapeDtypeStruct((b, t, width), F32)] * 7,
        scratch_shapes=[pltpu.VMEM((1, uw), F32)],
        compiler_params=_cparams(("parallel", "arbitrary")),
        name="rwkv_prep",
    )(u, shift_prev, mu, w0, a0, k_k, k_a, w2p, a2p, g2p)


def _rwkv_chunk_kernel(r_ref, lw_ref, k_ref, v_ref, nkk_ref, b_ref, g_ref, h0_ref,
                       rk_ref, lng_ref, lnb_ref, y_ref, hout_ref, h_scr, *, n_bat, n_pairs, n_chunks):
    ti = pl.program_id(1)
    c_len = CHUNK

    @pl.when(ti == 0)
    def _():
        h_scr[...] = h0_ref[...]

    lane = lax.broadcasted_iota(jnp.int32, (1, LANES), 1)
    head_lanes = (lane < HEAD_DIM, lane >= HEAD_DIM)
    t_idx = lax.broadcasted_iota(jnp.int32, (c_len, c_len), 0)
    i_idx = lax.broadcasted_iota(jnp.int32, (c_len, c_len), 1)
    strict = i_idx < t_idx
    causal = i_idx <= t_idx
    prefix = causal.astype(BF16)
    eye_c = (i_idx == t_idx).astype(F32)
    r128 = lax.broadcasted_iota(jnp.int32, (LANES, LANES), 0)
    c128 = lax.broadcasted_iota(jnp.int32, (LANES, LANES), 1)
    same_head = (r128 // HEAD_DIM) == (c128 // HEAD_DIM)
    eye128 = r128 == c128
    ones = _head_ones()
    n_sub = HEADS_PER_VREG

    def chunk_group(bi, r0):
        rows = pl.ds(r0, c_len)
        pairs = range(n_pairs)
        cols = [slice(pp * LANES, (pp + 1) * LANES) for pp in pairs]
        heads = [(pp, s) for pp in pairs for s in range(n_sub)]

        cums = []
        for pp in pairs:
            parts = _split_bf16(lw_ref[bi, rows, cols[pp]], 3)
            cums.append(_dot(prefix, parts[0]) + _dot(prefix, parts[1]) + _dot(prefix, parts[2]))

        a_ts, r_ts, b16s, k16s, v16s, bks, p_ends = [], [], [], [], [], [], []
        for pp in pairs:
            cs = cols[pp]
            cum = cums[pp]
            lw = lw_ref[bi, rows, cs]
            kf = k_ref[bi, rows, cs]
            bb = b_ref[bi, rows, cs]
            p_inv = jnp.exp(-cum)
            p_end = cum[c_len - 1:c_len, :]
            to_end = jnp.exp(p_end - cum)
            a_ts.append(nkk_ref[bi, rows, cs] * jnp.exp(cum - lw))
            r_ts.append(r_ref[bi, rows, cs] * jnp.exp(cum))
            b16s.append((bb * p_inv).astype(BF16))
            k16s.append((kf * p_inv).astype(BF16))
            v16s.append(v_ref[bi, rows, cs].astype(BF16))
            bks.append(jnp.concatenate([bb * to_end, kf * to_end], axis=0).astype(BF16))
            p_ends.append(p_end)

        m_bs, m_ks = [], []
        for pp, s in heads:
            sel = head_lanes[s]
            lhs = jnp.concatenate([jnp.where(sel, a_ts[pp], 0.0), jnp.where(sel, r_ts[pp], 0.0)],
                                  axis=0).astype(BF16)
            m_bs.append(_dot_nt(lhs, b16s[pp]))
            m_ks.append(_dot_nt(lhs, k16s[pp]))

        a_rbs = [jnp.where(causal, m[c_len:], 0.0).astype(BF16) for m in m_bs]
        a_rks = [jnp.where(causal, m[c_len:], 0.0).astype(BF16) for m in m_ks]
        avs = [_dot(jnp.where(strict, m[:c_len], 0.0).astype(BF16), v16s[pp])
               for m, (pp, _) in zip(m_ks, heads)]
        n_pows = [jnp.where(strict, m[:c_len], 0.0) for m in m_bs]
        t_invs = [eye_c + n for n in n_pows]
        span = 2
        while span < c_len:
            n16s = [n.astype(BF16) for n in n_pows]
            n_pows = [_dot(n, n) for n in n16s]
            t_invs = [_dot(t.astype(BF16), (eye_c + n).astype(BF16)) for t, n in zip(t_invs, n_pows)]
            span *= 2

        wus = [_dot(t.astype(BF16), jnp.concatenate([a_ts[pp], av], axis=1).astype(BF16))
               for t, av, (pp, _) in zip(t_invs, avs, heads)]
        zqs = [_dot(a, wu.astype(BF16)) for a, wu in zip(a_rbs, wus)]
        yks = [_dot(a, v16s[pp]) for a, (pp, _) in zip(a_rks, heads)]

        ys, gjs = [], []
        for pp in pairs:
            h0, h1 = n_sub * pp, n_sub * pp + 1
            sel_a = head_lanes[0]
            w2 = jnp.where(sel_a, wus[h0][:, :LANES], wus[h1][:, :LANES])
            u0 = jnp.where(sel_a, wus[h0][:, LANES:], wus[h1][:, LANES:])
            q2 = r_ts[pp] + jnp.where(sel_a, zqs[h0][:, :LANES], zqs[h1][:, :LANES])
            y0 = jnp.where(sel_a, zqs[h0][:, LANES:] + yks[h0], zqs[h1][:, LANES:] + yks[h1])
            h16 = h_scr[bi, pp].astype(BF16)
            ys.append(_dot(q2.astype(BF16), h16) + y0)
            top = jnp.concatenate([w2, u0], axis=1).astype(BF16)
            bot = jnp.concatenate([jnp.zeros((c_len, LANES), BF16), v16s[pp]], axis=1)
            gjs.append(_dot_tn(bks[pp], jnp.concatenate([top, bot], axis=0)))

        for pp in pairs:
            gj = gjs[pp]
            g_mat = jnp.where(eye128, jnp.exp(p_ends[pp]), 0.0) + jnp.where(same_head, gj[:, :LANES], 0.0)
            h16 = h_scr[bi, pp].astype(BF16)
            h_scr[bi, pp] = _dot(g_mat.astype(BF16), h16) + jnp.where(same_head, gj[:, LANES:], 0.0)

        means = [_head_sum(y, ones) * (1.0 / HEAD_DIM) for y in ys]
        ycs = [y - m for y, m in zip(ys, means)]
        variances = [_head_sum(yc * yc, ones) * (1.0 / HEAD_DIM) for yc in ycs]
        bonuses = [_head_sum(r_ref[bi, rows, cols[pp]] * k_ref[bi, rows, cols[pp]] * rk_ref[:, cols[pp]], ones)
                   for pp in pairs]
        for pp in pairs:
            cs = cols[pp]
            yn = ycs[pp] * lax.rsqrt(variances[pp] + GN_EPS) * lng_ref[:, cs] + lnb_ref[:, cs]
            y_ref[bi, rows, cs] = (yn + bonuses[pp] * v_ref[bi, rows, cs]) * g_ref[bi, rows, cs]

    def chunk_step(ci, carry):
        r0 = pl.multiple_of(ci * c_len, c_len)
        for bi in range(n_bat):
            chunk_group(bi, r0)
        return carry

    lax.fori_loop(0, n_chunks, chunk_step, 0)

    @pl.when(ti == pl.num_programs(1) - 1)
    def _():
        hout_ref[...] = h_scr[...]


def _rwkv_chunk(r, lw, kf, v, nkk, bb, g, h0, r_k, ln_g, ln_b):
    b, t, w = r.shape
    pairs = w // LANES
    tt = min(2 * CHUNK, t)
    n_bat = 2 if b % 2 == 0 else 1
    assert t % tt == 0 and tt % CHUNK == 0 and b % n_bat == 0
    seq = pl.BlockSpec((n_bat, tt, w), lambda bi, ti: (bi, ti, 0))
    par = pl.BlockSpec((1, w), lambda bi, ti: (0, 0))
    st = pl.BlockSpec((n_bat, pairs, LANES, LANES), lambda bi, ti: (bi, 0, 0, 0))
    return pl.pallas_call(
        functools.partial(_rwkv_chunk_kernel, n_bat=n_bat, n_pairs=pairs, n_chunks=tt // CHUNK),
        grid=(b // n_bat, t // tt),
        in_specs=[seq] * 7 + [st, par, par, par],
        out_specs=[seq, st],
        out_shape=[jax.ShapeDtypeStruct((b, t, w), F32),
                   jax.ShapeDtypeStruct((b, pairs, LANES, LANES), F32)],
        scratch_shapes=[pltpu.VMEM((n_bat, pairs, LANES, LANES), F32)],
        compiler_params=_cparams(("parallel", "arbitrary")),
        name="rwkv_chunk",
    )(r, lw, kf, v, nkk, bb, g, h0, r_k, ln_g, ln_b)


def _state_to_blockdiag(s):
    b, h, dv, dk = s.shape
    st = jnp.swapaxes(s, -1, -2).reshape(b, h // 2, 2, dk, dv)
    z = jnp.zeros_like(st[:, :, 0])
    top = jnp.concatenate([st[:, :, 0], z], axis=-1)
    bot = jnp.concatenate([z, st[:, :, 1]], axis=-1)
    return jnp.concatenate([top, bot], axis=-2)


def _blockdiag_to_state(hm):
    b, p, _, _ = hm.shape
    a = hm[:, :, :HEAD_DIM, :HEAD_DIM]
    c = hm[:, :, HEAD_DIM:, HEAD_DIM:]
    st = jnp.stack([a, c], axis=2).reshape(b, 2 * p, HEAD_DIM, HEAD_DIM)
    return jnp.swapaxes(st, -1, -2)


def _out_proj_kernel(x_ref, yr_ref, o_ref, w_ref, g_ref, rw_ref, rb_ref,
                     x1_ref, h2_ref, e_ref, gate_ref):
    half = yr_ref.shape[1]
    x1 = (x_ref[...] + _dot(yr_ref[...].astype(BF16), w_ref[:half, :])
          + _dot(o_ref[...].astype(BF16), w_ref[half:, :]))
    x1_ref[...] = x1
    ms = jnp.mean(x1 * x1, axis=-1, keepdims=True)
    h2 = x1 * lax.rsqrt(ms + RMS_EPS) * g_ref[...]
    h2_ref[...] = h2
    logits = jnp.dot(h2, rw_ref[...], preferred_element_type=F32,
                     precision=lax.Precision.HIGHEST) + rb_ref[...]
    lane = lax.broadcasted_iota(jnp.int32, logits.shape, 1)
    lane_f = lane.astype(F32)
    experts = jnp.zeros(logits.shape, F32)
    tops = jnp.zeros(logits.shape, F32)
    work = logits
    top0 = None
    for kth in range(TOP_K):
        m = jnp.max(work, axis=-1, keepdims=True)
        idx = jnp.min(jnp.where(work == m, lane_f, float(LANES)), axis=-1, keepdims=True)
        if top0 is None:
            top0 = m
        experts = jnp.where(lane == kth, idx, experts)
        tops = jnp.where(lane == kth, jnp.exp(m - top0), tops)
        work = jnp.where(lane_f == idx, -jnp.inf, work)
    e_ref[...] = experts.astype(jnp.int32)
    gate_ref[...] = tops / jnp.sum(tops, axis=-1, keepdims=True)


def _out_proj(x2, yr, o, w_out, norm_g, rw, rb):
    n, d = x2.shape
    tm = min(256, n)
    assert n % tm == 0
    half = yr.shape[1]
    row = lambda i: (i, 0)
    fixed = lambda i: (0, 0)
    return pl.pallas_call(
        _out_proj_kernel,
        grid=(n // tm,),
        in_specs=[
            pl.BlockSpec((tm, d), row),
            pl.BlockSpec((tm, half), row),
            pl.BlockSpec((tm, half), row),
            pl.BlockSpec(w_out.shape, fixed),
            pl.BlockSpec((1, d), fixed),
            pl.BlockSpec(rw.shape, fixed),
            pl.BlockSpec((1, LANES), fixed),
        ],
        out_specs=[
            pl.BlockSpec((tm, d), row),
            pl.BlockSpec((tm, d), row),
            pl.BlockSpec((tm, LANES), row),
            pl.BlockSpec((tm, LANES), row),
        ],
        out_shape=[
            jax.ShapeDtypeStruct((n, d), F32),
            jax.ShapeDtypeStruct((n, d), F32),
            jax.ShapeDtypeStruct((n, LANES), jnp.int32),
            jax.ShapeDtypeStruct((n, LANES), F32),
        ],
        compiler_params=_cparams(("parallel",)),
        name="out_proj",
    )(x2, yr, o, w_out, norm_g, rw, rb)


def _dispatch_kernel(nv_ref, dest_ref, h2_ref, xbuf_hbm, zero_scr, sem, *, tokens, n_blocks):
    @pl.when(pl.program_id(0) == 0)
    def _():
        zero_scr[...] = jnp.zeros_like(zero_scr)

        def fill(i):
            r0 = pl.multiple_of(i * MOE_BLOCK, MOE_BLOCK)
            return pltpu.make_async_copy(zero_scr, xbuf_hbm.at[pl.ds(r0, MOE_BLOCK)], sem)

        def fill_start(i, c):
            @pl.when(nv_ref[i] < MOE_BLOCK)
            def _():
                fill(i).start()
            return c

        def fill_wait(i, c):
            @pl.when(nv_ref[i] < MOE_BLOCK)
            def _():
                fill(i).wait()
            return c

        lax.fori_loop(0, n_blocks, fill_start, 0)
        lax.fori_loop(0, n_blocks, fill_wait, 0)

    def copy(a):
        return pltpu.make_async_copy(h2_ref.at[pl.ds(a // TOP_K, 1)],
                                     xbuf_hbm.at[pl.ds(dest_ref[0, 0, a], 1)], sem)

    def start(a, c):
        copy(a).start()
        return c

    def wait(a, c):
        copy(a).wait()
        return c

    lax.fori_loop(0, tokens * TOP_K, start, 0, unroll=8)
    lax.fori_loop(0, tokens * TOP_K, wait, 0, unroll=8)


def _dispatch(h2, dest, n_valid):
    n, d = h2.shape
    tokens = 128
    rows = tokens * TOP_K
    assert n % tokens == 0
    steps = n // tokens
    n_blocks = n_valid.shape[0]
    grid_spec = pltpu.PrefetchScalarGridSpec(
        num_scalar_prefetch=1,
        grid=(steps,),
        in_specs=[
            pl.BlockSpec((1, 1, rows), lambda i, nv: (i, 0, 0), memory_space=pltpu.SMEM),
            pl.BlockSpec((tokens, d), lambda i, nv: (i, 0)),
        ],
        out_specs=pl.BlockSpec(memory_space=pl.ANY),
        scratch_shapes=[pltpu.VMEM((MOE_BLOCK, d), F32), pltpu.SemaphoreType.DMA],
    )
    return pl.pallas_call(
        functools.partial(_dispatch_kernel, tokens=tokens, n_blocks=n_blocks),
        grid_spec=grid_spec,
        out_shape=jax.ShapeDtypeStruct((n_blocks * MOE_BLOCK, d), F32),
        compiler_params=_cparams(("arbitrary",)),
        name="moe_dispatch",
    )(n_valid, dest.reshape(steps, 1, rows), h2)


def _moe_ffn_kernel(be_ref, nv_ref, x_ref, w1g_ref, w1l_ref, b1g_ref, b1l_ref, w2_ref, b2_ref,
                    y_ref, xs_scr):
    i = pl.program_id(0)
    f = pl.program_id(1)
    n_valid = nv_ref[i]

    @pl.when(n_valid == 0)
    def _():
        y_ref[...] = jnp.zeros_like(y_ref)

    @pl.when(n_valid > 0)
    def _():
        @pl.when(f == 0)
        def _():
            xs_scr[...] = x_ref[...].astype(BF16)

        xs = xs_scr[...]
        hg = _dot(xs, w1g_ref[0].astype(BF16)) + b1g_ref[0]
        hl = _dot(xs, w1l_ref[0].astype(BF16)) + b1l_ref[0]
        glu = jnp.minimum(hg, SWIGLU_LIMIT)
        lin = jnp.clip(hl, -SWIGLU_LIMIT, SWIGLU_LIMIT)
        act = glu * _sigmoid(SWIGLU_ALPHA * glu) * (lin + 1.0)
        part = _dot(act.astype(BF16), w2_ref[0].astype(BF16))

        @pl.when(f == 0)
        def _():
            y_ref[...] = part + b2_ref[0]

        @pl.when(f > 0)
        def _():
            y_ref[...] += part


def _moe_ffn(xbuf, block_e, n_valid, w1, b1, w2, b2):
    n_rows, d = xbuf.shape
    n_exp, _, ff2 = w1.shape
    ff = ff2 // 2
    bm, tf = MOE_BLOCK, MOE_FF_TILE
    n_f = ff // tf
    n_blocks = n_rows // bm
    b1r = b1.reshape(n_exp, 1, ff2)
    b2r = b2.reshape(n_exp, 1, d)
    grid_spec = pltpu.PrefetchScalarGridSpec(
        num_scalar_prefetch=2,
        grid=(n_blocks, n_f),
        in_specs=[
            pl.BlockSpec((bm, d), lambda i, f, be, nv: (i, 0)),
            pl.BlockSpec((1, d, tf), lambda i, f, be, nv: (be[i], 0, f)),
            pl.BlockSpec((1, d, tf), lambda i, f, be, nv: (be[i], 0, n_f + f)),
            pl.BlockSpec((1, 1, tf), lambda i, f, be, nv: (be[i], 0, f)),
            pl.BlockSpec((1, 1, tf), lambda i, f, be, nv: (be[i], 0, n_f + f)),
            pl.BlockSpec((1, tf, d), lambda i, f, be, nv: (be[i], f, 0)),
            pl.BlockSpec((1, 1, d), lambda i, f, be, nv: (be[i], 0, 0)),
        ],
        out_specs=pl.BlockSpec((bm, d), lambda i, f, be, nv: (i, 0)),
        scratch_shapes=[pltpu.VMEM((bm, d), BF16)],
    )
    return pl.pallas_call(
        _moe_ffn_kernel,
        grid_spec=grid_spec,
        out_shape=jax.ShapeDtypeStruct((n_rows, d), F32),
        compiler_params=_cparams(("parallel", "arbitrary")),
        name="moe_ffn",
    )(block_e, n_valid, xbuf, w1, w1, b1r, b1r, w2, b2r)


def _combine_kernel(dest_ref, x1_ref, gate_ref, ybuf_hbm, out_ref, rows_scr, sem, *, tokens):
    def copy(a):
        return pltpu.make_async_copy(ybuf_hbm.at[pl.ds(dest_ref[0, 0, a], 1)],
                                     rows_scr.at[a % TOP_K, pl.ds(a // TOP_K, 1)], sem)

    def start(a, c):
        copy(a).start()
        return c

    def wait(a, c):
        copy(a).wait()
        return c

    lax.fori_loop(0, tokens * TOP_K, start, 0, unroll=8)
    lax.fori_loop(0, tokens * TOP_K, wait, 0, unroll=8)
    acc = x1_ref[...]
    gate = gate_ref[...]
    for kth in range(TOP_K):
        acc = acc + gate[:, kth:kth + 1] * rows_scr[kth]
    out_ref[...] = acc


def _combine(x1, gate, dest, ybuf):
    n, d = x1.shape
    tokens = min(GATHER_TOKENS, n)
    assert n % tokens == 0
    steps = n // tokens
    return pl.pallas_call(
        functools.partial(_combine_kernel, tokens=tokens),
        grid=(steps,),
        in_specs=[
            pl.BlockSpec((1, 1, tokens * TOP_K), lambda i: (i, 0, 0), memory_space=pltpu.SMEM),
            pl.BlockSpec((tokens, d), lambda i: (i, 0)),
            pl.BlockSpec((tokens, LANES), lambda i: (i, 0)),
            pl.BlockSpec(memory_space=pl.ANY),
        ],
        out_specs=pl.BlockSpec((tokens, d), lambda i: (i, 0)),
        out_shape=jax.ShapeDtypeStruct((n, d), F32),
        scratch_shapes=[pltpu.VMEM((TOP_K, tokens, d), F32), pltpu.SemaphoreType.DMA],
        compiler_params=_cparams(("arbitrary",)),
        name="moe_combine",
    )(dest.reshape(steps, 1, tokens * TOP_K), x1, gate, ybuf)


def _route(experts, n_exp, n_blocks):
    flat_e = experts.reshape(-1)
    onehot = (flat_e[:, None] == jnp.arange(n_exp, dtype=jnp.int32)[None, :]).astype(jnp.int32)
    csum = jnp.cumsum(onehot, axis=0)
    counts = csum[-1]
    padded = (counts + MOE_BLOCK - 1) // MOE_BLOCK * MOE_BLOCK
    pend = jnp.cumsum(padded)
    pstart = pend - padded
    dest = jnp.sum(onehot * (csum - 1 + pstart[None, :]), axis=1)
    block_start = jnp.arange(n_blocks, dtype=jnp.int32) * MOE_BLOCK
    block_e = jnp.minimum(jnp.sum(block_start[:, None] >= pend[None, :], axis=1), n_exp - 1)
    n_valid = jnp.clip(counts[block_e] - (block_start - pstart[block_e]), 0, MOE_BLOCK)
    return dest.astype(jnp.int32), block_e.astype(jnp.int32), n_valid.astype(jnp.int32)


def _mixers(x, past_k, past_v, s0, shift_prev, wts):
    b, t, d = x.shape
    width = wts["width"]
    u, q, k, v = _in_proj(x.reshape(b * t, d), wts["norm1_g"], wts["w_all"], wts["qg"], wts["kg"],
                          wts["n_u"], wts["n_h"])
    uw = u.shape[1]
    u = u.reshape(b, t, uw)
    q, k, v = (a.reshape(b, t, width) for a in (q, k, v))
    o = _sb_attend(q, k, v, wts["sb_o_g"], past_k, past_v)

    shift_p = jnp.pad(shift_prev, ((0, 0), (0, uw - shift_prev.shape[1]))).reshape(b, 1, uw)
    r, lw, kf, vv, nkk, bb, g = _rwkv_prep(u, shift_p, wts["mu"], wts["w0"], wts["a0"], wts["k_k"],
                                           wts["k_a"], wts["w2p"], wts["a2p"], wts["g2p"], width)
    y_rwkv, h_fin = _rwkv_chunk(r, lw, kf, vv, nkk, bb, g, _state_to_blockdiag(s0),
                                wts["r_k"], wts["ln_g"], wts["ln_b"])
    x1, h2, experts, gates = _out_proj(x.reshape(b * t, d), y_rwkv.reshape(b * t, width),
                                       o.reshape(b * t, width), wts["w_out"], wts["norm2_g"],
                                       wts["rw"], wts["rb"])
    new_k = k.reshape(b, t, width // HEAD_DIM, HEAD_DIM)
    new_v = v.reshape(b, t, width // HEAD_DIM, HEAD_DIM)
    new_shift = u[:, -1, :wts["rwkv_in"]]
    return x1, h2, experts[:, :TOP_K], gates, new_k, new_v, _blockdiag_to_state(h_fin), new_shift


def kernel(x_prompt, x_sample, cache_sb_k, cache_sb_v, state_rwkv, state_shift, norm1_g, w_in, rwkv_mu, rwkv_w0, rwkv_w2, rwkv_a0, rwkv_a2, rwkv_g2, rwkv_k_k, rwkv_k_a, rwkv_r_k, rwkv_ln_g, rwkv_ln_b, sb_q_g, sb_k_g, sb_o_g, w_out, norm2_g, router_w, router_b, moe_w1, moe_b1, moe_w2, moe_b2):
    depth = w_in.shape[0]
    assert depth == 1
    layer = 0
    d = x_prompt.shape[-1]
    width = rwkv_w0.shape[-1]
    rwkv_in = rwkv_mu.shape[-1]
    n_exp = router_w.shape[-1]
    decay_lora, aaa_lora, gate_lora = rwkv_w2.shape[1], rwkv_a2.shape[1], rwkv_g2.shape[1]
    assert decay_lora + aaa_lora == LANES and width % PROJ_TILE == 0
    n_h = width // PROJ_TILE
    n_u = -(-rwkv_in // PROJ_TILE)
    uw = n_u * PROJ_TILE
    tail_w = uw - 3 * width

    w_l = w_in[layer]
    w_all = jnp.concatenate(
        [w_l[:, :rwkv_in], jnp.zeros((d, uw - rwkv_in), F32), w_l[:, rwkv_in:]], axis=1).astype(BF16)
    reps = PROJ_TILE // HEAD_DIM
    row = lambda a: a.reshape(1, -1).astype(F32)
    wts = dict(
        width=width, rwkv_in=rwkv_in, n_u=n_u, n_h=n_h,
        norm1_g=row(norm1_g[layer]), w_all=w_all,
        qg=row(jnp.tile(sb_q_g[layer], reps)), kg=row(jnp.tile(sb_k_g[layer], reps)),
        sb_o_g=row(sb_o_g[layer]),
        mu=jnp.pad(row(rwkv_mu[layer]), ((0, 0), (0, uw - rwkv_in))),
        w0=row(rwkv_w0[layer]), a0=row(rwkv_a0[layer]),
        k_k=row(rwkv_k_k[layer]), k_a=row(rwkv_k_a[layer]),
        w2p=jnp.pad(rwkv_w2[layer], ((0, aaa_lora), (0, 0))).astype(BF16),
        a2p=jnp.pad(rwkv_a2[layer], ((decay_lora, 0), (0, 0))).astype(BF16),
        g2p=jnp.pad(rwkv_g2[layer], ((0, tail_w - LANES - gate_lora), (0, 0))).astype(BF16),
        r_k=row(rwkv_r_k[layer]), ln_g=row(rwkv_ln_g[layer]), ln_b=row(rwkv_ln_b[layer]),
        w_out=w_out[layer].astype(BF16), norm2_g=row(norm2_g[layer]),
        rw=jnp.pad(router_w[layer], ((0, 0), (0, LANES - n_exp))),
        rb=jnp.pad(row(router_b[layer]), ((0, 0), (0, LANES - n_exp)), constant_values=-jnp.inf),
    )

    bp, tp, _ = x_prompt.shape
    bs, ts, _ = x_sample.shape
    heads = width // HEAD_DIM
    s0_prompt = jnp.zeros((bp, heads, HEAD_DIM, HEAD_DIM), F32)
    shift0_prompt = jnp.zeros((bp, rwkv_in), F32)
    past_k = cache_sb_k[layer].reshape(bs, -1, width)
    past_v = cache_sb_v[layer].reshape(bs, -1, width)

    grp_p = _mixers(x_prompt, None, None, s0_prompt, shift0_prompt, wts)
    grp_s = _mixers(x_sample, past_k, past_v, state_rwkv[layer], state_shift[layer], wts)

    n_p, n_s = bp * tp, bs * ts
    n_tok = n_p + n_s
    h2 = jnp.concatenate([grp_p[1], grp_s[1]], axis=0)
    experts = jnp.concatenate([grp_p[2], grp_s[2]], axis=0)
    n_blocks = -(-(n_tok * TOP_K) // MOE_BLOCK) + n_exp
    dest, block_e, n_valid = _route(experts, n_exp, n_blocks)
    xbuf = _dispatch(h2, dest, n_valid)
    ybuf = _moe_ffn(xbuf, block_e, n_valid, moe_w1[layer], moe_b1[layer], moe_w2[layer], moe_b2[layer])
    y_p = _combine(grp_p[0], grp_p[3], dest[:n_p * TOP_K], ybuf).reshape(bp, tp, d)
    y_s = _combine(grp_s[0], grp_s[3], dest[n_p * TOP_K:], ybuf).reshape(bs, ts, d)

    lead = lambda a: a[None]
    return (y_p, y_s,
            lead(grp_p[4]), lead(grp_p[5]), lead(grp_p[6]), lead(grp_p[7]),
            lead(grp_s[4]), lead(grp_s[5]), lead(grp_s[6]), lead(grp_s[7]))
```

```python
import functools

import jax
import jax.numpy as jnp
from jax import lax
from jax.experimental import pallas as pl
from jax.experimental.pallas import tpu as pltpu

F32 = jnp.float32
BF16 = jnp.bfloat16

LANES = 128
SUBLANES = 8
VMEM_LIMIT_BYTES = 56 * 1024 * 1024

HEAD_DIM = 64
HEADS_PER_VREG = LANES // HEAD_DIM
TOP_K = 4
RMS_EPS = 1e-5
GN_EPS = 64e-5
SWIGLU_ALPHA = 1.702
SWIGLU_LIMIT = 7.0
KK_NORM_FLOOR = 1e-12

PROJ_TILE = 512
ROW_TILE = 512
CHUNK = 64
SB_BLOCK = 128
SB_DEAD_LOGIT = -104.0
MOE_BLOCK = 512
MOE_FF_TILE = 512
GATHER_TOKENS = 64


def _cparams(sem):
    return pltpu.CompilerParams(dimension_semantics=sem, vmem_limit_bytes=VMEM_LIMIT_BYTES)


def _sigmoid(x):
    return 1.0 / (1.0 + jnp.exp(-x))


def _softplus(x):
    return jnp.maximum(x, 0.0) + jnp.log(1.0 + jnp.exp(-jnp.abs(x)))


def _split_bf16(x, parts):
    out = []
    rem = x
    for _ in range(parts):
        p = rem.astype(BF16)
        out.append(p)
        rem = rem - p.astype(F32)
    return out


def _dot(a, b):
    return jnp.dot(a, b, preferred_element_type=F32)


def _dot_nt(a, b):
    return lax.dot_general(a, b, (((1,), (1,)), ((), ())), preferred_element_type=F32)


def _dot_tn(a, b):
    return lax.dot_general(a, b, (((0,), (0,)), ((), ())), preferred_element_type=F32)


def _head_ones(n=LANES):
    r = lax.broadcasted_iota(jnp.int32, (n, n), 0) // HEAD_DIM
    c = lax.broadcasted_iota(jnp.int32, (n, n), 1) // HEAD_DIM
    return (r == c).astype(BF16)


def _head_sum(x, ones, parts=1):
    acc = None
    for p in _split_bf16(x, parts):
        t = _dot(p, ones)
        acc = t if acc is None else acc + t
    return acc


def _in_proj_kernel(x_ref, g_ref, w_ref, qg_ref, kg_ref, u_ref, q_ref, k_ref, v_ref, k16_ref, v16_ref,
                    h_scr, *, n_u, n_h):
    j = pl.program_id(1)

    @pl.when(j == 0)
    def _():
        x = x_ref[...]
        ms = jnp.mean(x * x, axis=-1, keepdims=True)
        h_scr[...] = (x * lax.rsqrt(ms + RMS_EPS) * g_ref[...]).astype(BF16)

    acc = _dot(h_scr[...], w_ref[...])

    def head_norm(y, gain):
        ones = _head_ones()
        cols = []
        for c in range(y.shape[1] // LANES):
            yc = y[:, c * LANES:(c + 1) * LANES]
            ms = _head_sum(yc * yc, ones) * (1.0 / HEAD_DIM)
            cols.append(yc * lax.rsqrt(ms + RMS_EPS))
        return jnp.concatenate(cols, axis=1) * gain

    @pl.when(j < n_u)
    def _():
        u_ref[...] = acc

    @pl.when((j >= n_u) & (j < n_u + n_h))
    def _():
        q_ref[...] = head_norm(acc, qg_ref[...])

    @pl.when((j >= n_u + n_h) & (j < n_u + 2 * n_h))
    def _():
        kn = head_norm(acc, kg_ref[...])
        k_ref[...] = kn
        k16_ref[...] = kn.astype(BF16)

    @pl.when(j >= n_u + 2 * n_h)
    def _():
        v_ref[...] = acc
        v16_ref[...] = acc.astype(BF16)


def _in_proj(x2, norm_g, w_all, qg, kg, n_u, n_h):
    n, d = x2.shape
    tm = min(ROW_TILE, n)
    tn = PROJ_TILE
    n_tiles = n_u + 3 * n_h
    assert w_all.shape == (d, n_tiles * tn) and n % tm == 0
    hw = n_h * tn

    def clip_map(lo, cnt):
        return lambda i, j: (i, jnp.clip(j - lo, 0, cnt - 1))

    return pl.pallas_call(
        functools.partial(_in_proj_kernel, n_u=n_u, n_h=n_h),
        grid=(n // tm, n_tiles),
        in_specs=[
            pl.BlockSpec((tm, d), lambda i, j: (i, 0)),
            pl.BlockSpec((1, d), lambda i, j: (0, 0)),
            pl.BlockSpec((d, tn), lambda i, j: (0, j)),
            pl.BlockSpec((1, tn), lambda i, j: (0, 0)),
            pl.BlockSpec((1, tn), lambda i, j: (0, 0)),
        ],
        out_specs=[
            pl.BlockSpec((tm, tn), clip_map(0, n_u)),
            pl.BlockSpec((tm, tn), clip_map(n_u, n_h)),
            pl.BlockSpec((tm, tn), clip_map(n_u + n_h, n_h)),
            pl.BlockSpec((tm, tn), clip_map(n_u + 2 * n_h, n_h)),
            pl.BlockSpec((tm, tn), clip_map(n_u + n_h, n_h)),
            pl.BlockSpec((tm, tn), clip_map(n_u + 2 * n_h, n_h)),
        ],
        out_shape=[
            jax.ShapeDtypeStruct((n, n_u * tn), F32),
            jax.ShapeDtypeStruct((n, hw), F32),
            jax.ShapeDtypeStruct((n, hw), F32),
            jax.ShapeDtypeStruct((n, hw), F32),
            jax.ShapeDtypeStruct((n, hw), BF16),
            jax.ShapeDtypeStruct((n, hw), BF16),
        ],
        scratch_shapes=[pltpu.VMEM((tm, d), BF16)],
        compiler_params=_cparams(("parallel", "arbitrary")),
        name="in_proj",
    )(x2, norm_g, w_all, qg, kg)


def _sb_kernel(*refs, tq, n_past, n_grp):
    if n_past:
        q_ref, k_ref, v_ref, pk_ref, pv_ref, og_ref, o_ref, acc_ref = refs
    else:
        q_ref, k_ref, v_ref, og_ref, o_ref, acc_ref = refs
        pk_ref = pv_ref = None
    qi = pl.program_id(2)
    n_heads = HEADS_PER_VREG * n_grp
    lane = lax.broadcasted_iota(jnp.int32, (1, LANES), 1)
    in_a = lane < HEAD_DIM
    q_heads = []
    for g in range(n_grp):
        q = q_ref[0, :, g * LANES:(g + 1) * LANES] * (HEAD_DIM ** -0.5)
        q_heads.append(jnp.where(in_a, q, 0.0).astype(BF16))
        q_heads.append(jnp.where(in_a, 0.0, q).astype(BF16))
    acc_ref[...] = jnp.zeros_like(acc_ref)

    def visit(k_src, v_src, r0, tk, carries, mask):
        rows = pl.ds(r0, tk)
        s_idx = lax.broadcasted_iota(jnp.int32, (tk, tk), 0)
        j_idx = lax.broadcasted_iota(jnp.int32, (tk, tk), 1)
        suffix = (s_idx >= j_idx).astype(BF16)
        kbs = [k_src[0, rows, g * LANES:(g + 1) * LANES] for g in range(n_grp)]
        zs = [_dot_nt(q_heads[h], kbs[h // HEADS_PER_VREG]) for h in range(n_heads)]
        splits = []
        for z in zs:
            log_1mb = -_softplus(z)
            if mask is not None:
                log_1mb = jnp.where(mask, log_1mb, 0.0)
            splits.append(_split_bf16(log_1mb, 2))
        incls = [_dot(hi, suffix) + _dot(lo, suffix) for hi, lo in splits]
        ps = []
        for z, incl, carry in zip(zs, incls, carries):
            p = jnp.exp(z + incl + carry)
            if mask is not None:
                p = jnp.where(mask, p, 0.0)
            ps.append(p.astype(BF16))
        vbs = [v_src[0, rows, g * LANES:(g + 1) * LANES] for g in range(n_grp)]
        for h in range(n_heads):
            acc_ref[h] += _dot(ps[h], vbs[h // HEADS_PER_VREG])
        return tuple(c + incl[:, 0:1] for c, incl in zip(carries, incls))

    q0 = pl.multiple_of(qi * tq, tq)
    t_idx = lax.broadcasted_iota(jnp.int32, (tq, tq), 0)
    j_idx = lax.broadcasted_iota(jnp.int32, (tq, tq), 1)
    zero = jnp.zeros((tq, 1), F32)
    carries = visit(k_ref, v_ref, q0, tq, (zero,) * n_heads, j_idx < t_idx)

    def alive(carries):
        top = carries[0]
        for c in carries[1:]:
            top = jnp.maximum(top, c)
        return (jnp.max(top) > SB_DEAD_LOGIT).astype(jnp.int32)

    def sweep(kr, vr, n_blocks, carries):
        def cond(st):
            return (st[0] >= 0) & (st[1] > 0)

        def body(st):
            b = st[0]
            r0 = pl.multiple_of(b * SB_BLOCK, SB_BLOCK)
            c = visit(kr, vr, r0, SB_BLOCK, st[2:], None)
            return (b - 1, alive(c)) + c

        st = lax.while_loop(cond, body, (n_blocks - 1, alive(carries)) + carries)
        return st[2:]

    if tq == SB_BLOCK:
        carries = sweep(k_ref, v_ref, qi, carries)
    if n_past:
        carries = sweep(pk_ref, pv_ref, jnp.int32(n_past), carries)

    ones = _head_ones()
    for g in range(n_grp):
        o = jnp.where(in_a, acc_ref[HEADS_PER_VREG * g], acc_ref[HEADS_PER_VREG * g + 1])
        ms = _head_sum(o * o, ones) * (1.0 / HEAD_DIM)
        cs = slice(g * LANES, (g + 1) * LANES)
        o_ref[0, :, cs] = o * lax.rsqrt(ms + RMS_EPS) * og_ref[:, cs]


def _sb_attend(q, k, v, og, past_k=None, past_v=None):
    b, t, w = q.shape
    n_grp = 4
    gw = n_grp * LANES
    assert w % gw == 0
    tq = min(SB_BLOCK, t)
    assert t % tq == 0 and (t == tq or tq == SB_BLOCK)
    n_past = 0
    in_specs = [
        pl.BlockSpec((1, tq, gw), lambda bi, p, qi: (bi, qi, p)),
        pl.BlockSpec((1, t, gw), lambda bi, p, qi: (bi, 0, p)),
        pl.BlockSpec((1, t, gw), lambda bi, p, qi: (bi, 0, p)),
    ]
    args = [q, k, v]
    if past_k is not None:
        tp = past_k.shape[1]
        assert tp % SB_BLOCK == 0 and t == tq
        n_past = tp // SB_BLOCK
        in_specs += [pl.BlockSpec((1, tp, gw), lambda bi, p, qi: (bi, 0, p))] * 2
        args += [past_k, past_v]
    in_specs.append(pl.BlockSpec((1, gw), lambda bi, p, qi: (0, p)))
    args.append(og)
    return pl.pallas_call(
        functools.partial(_sb_kernel, tq=tq, n_past=n_past, n_grp=n_grp),
        grid=(b, w // gw, t // tq),
        in_specs=in_specs,
        out_specs=pl.BlockSpec((1, tq, gw), lambda bi, p, qi: (bi, qi, p)),
        out_shape=jax.ShapeDtypeStruct((b, t, w), F32),
        scratch_shapes=[pltpu.VMEM((HEADS_PER_VREG * n_grp, tq, LANES), F32)],
        compiler_params=_cparams(("parallel", "parallel", "arbitrary")),
        name="sb_attend",
    )(*args)


def _rwkv_prep_kernel(u_ref, sh_ref, mu_ref, w0_ref, a0_ref, kk_ref, ka_ref, w2_ref, a2_ref, g2_ref,
                      r_o, lw_o, kf_o, v_o, nkk_o, b_o, g_o, prev_scr, *, width):
    ti = pl.program_id(1)
    tt = u_ref.shape[1]

    @pl.when(ti == 0)
    def _():
        prev_scr[...] = sh_ref[0]

    row = lax.broadcasted_iota(jnp.int32, (tt, 1), 0)

    def mixed(lo, hi):
        u_raw = u_ref[0, :, lo:hi]
        shifted = jnp.where(row == 0, prev_scr[:, lo:hi], pltpu.roll(u_raw, 1, axis=0))
        return u_raw + (shifted - u_raw) * mu_ref[:, lo:hi]

    r = mixed(0, width)
    k = mixed(width, 2 * width)
    v = mixed(2 * width, 3 * width)
    tail = mixed(3 * width, u_ref.shape[2])
    prev_scr[...] = u_ref[0, tt - 1:tt, :]

    lora_in = tail[:, :LANES]
    dw = _dot(jnp.tanh(lora_in).astype(BF16), w2_ref[...])
    da = _dot(lora_in.astype(BF16), a2_ref[...])
    g = _dot(_sigmoid(tail[:, LANES:]).astype(BF16), g2_ref[...])
    w_log = -_softplus(-(w0_ref[...] + dw)) - 0.5
    a = _sigmoid(a0_ref[...] + da)

    kk = k * kk_ref[...]
    ones = _head_ones()
    cols = []
    for c in range(width // LANES):
        kc = kk[:, c * LANES:(c + 1) * LANES]
        norm = jnp.sqrt(_head_sum(kc * kc, ones))
        cols.append(kc / jnp.maximum(norm, KK_NORM_FLOOR))
    kk = jnp.concatenate(cols, axis=1)

    r_o[0] = r
    lw_o[0] = -jnp.exp(w_log)
    kf_o[0] = k * (1.0 + (a - 1.0) * ka_ref[...])
    v_o[0] = v
    nkk_o[0] = -kk
    b_o[0] = kk * a
    g_o[0] = g


def _rwkv_prep(u, shift_prev, mu, w0, a0, k_k, k_a, w2p, a2p, g2p, width):
    b, t, uw = u.shape
    tt = min(256, t)
    assert t % tt == 0
    row_spec = pl.BlockSpec((1, width), lambda bi, ti: (0, 0))
    out_spec = pl.BlockSpec((1, tt, width), lambda bi, ti: (bi, ti, 0))
    return pl.pallas_call(
        functools.partial(_rwkv_prep_kernel, width=width),
        grid=(b, t // tt),
        in_specs=[
            pl.BlockSpec((1, tt, uw), lambda bi, ti: (bi, ti, 0)),
            pl.BlockSpec((1, 1, uw), lambda bi, ti: (bi, 0, 0)),
            pl.BlockSpec((1, uw), lambda bi, ti: (0, 0)),
            row_spec, row_spec, row_spec, row_spec,
            pl.BlockSpec(w2p.shape, lambda bi, ti: (0, 0)),
            pl.BlockSpec(a2p.shape, lambda bi, ti: (0, 0)),
            pl.BlockSpec(g2p.shape, lambda bi, ti: (0, 0)),
        ],
        out_specs=[out_spec] * 7,
        out_shape=[jax.ShapeDtypeStruct((b, t, width), F32)] * 7,
        scratch_shapes=[pltpu.VMEM((1, uw), F32)],
        compiler_params=_cparams(("parallel", "arbitrary")),
        name="rwkv_prep",
    )(u, shift_prev, mu, w0, a0, k_k, k_a, w2p, a2p, g2p)


def _rwkv_chunk_kernel(r_ref, lw_ref, k_ref, v_ref, nkk_ref, b_ref, g_ref, h0_ref,
                       rk_ref, lng_ref, lnb_ref, y_ref, hout_ref, h_scr, *, n_bat, n_pairs, n_chunks):
    ti = pl.program_id(1)
    c_len = CHUNK

    @pl.when(ti == 0)
    def _():
        h_scr[...] = h0_ref[...]

    lane = lax.broadcasted_iota(jnp.int32, (1, LANES), 1)
    head_lanes = (lane < HEAD_DIM, lane >= HEAD_DIM)
    t_idx = lax.broadcasted_iota(jnp.int32, (c_len, c_len), 0)
    i_idx = lax.broadcasted_iota(jnp.int32, (c_len, c_len), 1)
    strict = i_idx < t_idx
    causal = i_idx <= t_idx
    prefix = causal.astype(BF16)
    eye_c = (i_idx == t_idx).astype(F32)
    r128 = lax.broadcasted_iota(jnp.int32, (LANES, LANES), 0)
    c128 = lax.broadcasted_iota(jnp.int32, (LANES, LANES), 1)
    same_head = (r128 // HEAD_DIM) == (c128 // HEAD_DIM)
    eye128 = r128 == c128
    ones = _head_ones()
    n_sub = HEADS_PER_VREG

    def chunk_group(bi, r0):
        rows = pl.ds(r0, c_len)
        pairs = range(n_pairs)
        cols = [slice(pp * LANES, (pp + 1) * LANES) for pp in pairs]
        heads = [(pp, s) for pp in pairs for s in range(n_sub)]

        cums = []
        for pp in pairs:
            parts = _split_bf16(lw_ref[bi, rows, cols[pp]], 3)
            cums.append(_dot(prefix, parts[0]) + _dot(prefix, parts[1]) + _dot(prefix, parts[2]))

        a_ts, r_ts, b16s, k16s, v16s, bks, p_ends = [], [], [], [], [], [], []
        for pp in pairs:
            cs = cols[pp]
            cum = cums[pp]
            lw = lw_ref[bi, rows, cs]
            kf = k_ref[bi, rows, cs]
            bb = b_ref[bi, rows, cs]
            p_inv = jnp.exp(-cum)
            p_end = cum[c_len - 1:c_len, :]
            to_end = jnp.exp(p_end - cum)
            a_ts.append(nkk_ref[bi, rows, cs] * jnp.exp(cum - lw))
            r_ts.append(r_ref[bi, rows, cs] * jnp.exp(cum))
            b16s.append((bb * p_inv).astype(BF16))
            k16s.append((kf * p_inv).astype(BF16))
            v16s.append(v_ref[bi, rows, cs].astype(BF16))
            bks.append(jnp.concatenate([bb * to_end, kf * to_end], axis=0).astype(BF16))
            p_ends.append(p_end)

        m_bs, m_ks = [], []
        for pp, s in heads:
            sel = head_lanes[s]
            lhs = jnp.concatenate([jnp.where(sel, a_ts[pp], 0.0), jnp.where(sel, r_ts[pp], 0.0)],
                                  axis=0).astype(BF16)
            m_bs.append(_dot_nt(lhs, b16s[pp]))
            m_ks.append(_dot_nt(lhs, k16s[pp]))

        a_rbs = [jnp.where(causal, m[c_len:], 0.0).astype(BF16) for m in m_bs]
        a_rks = [jnp.where(causal, m[c_len:], 0.0).astype(BF16) for m in m_ks]
        avs = [_dot(jnp.where(strict, m[:c_len], 0.0).astype(BF16), v16s[pp])
               for m, (pp, _) in zip(m_ks, heads)]
        n_pows = [jnp.where(strict, m[:c_len], 0.0) for m in m_bs]
        t_invs = [eye_c + n for n in n_pows]
        span = 2
        while span < c_len:
            n16s = [n.astype(BF16) for n in n_pows]
            n_pows = [_dot(n, n) for n in n16s]
            t_invs = [_dot(t.astype(BF16), (eye_c + n).astype(BF16)) for t, n in zip(t_invs, n_pows)]
            span *= 2

        wus = [_dot(t.astype(BF16), jnp.concatenate([a_ts[pp], av], axis=1).astype(BF16))
               for t, av, (pp, _) in zip(t_invs, avs, heads)]
        zqs = [_dot(a, wu.astype(BF16)) for a, wu in zip(a_rbs, wus)]
        yks = [_dot(a, v16s[pp]) for a, (pp, _) in zip(a_rks, heads)]

        ys, gjs = [], []
        for pp in pairs:
            h0, h1 = n_sub * pp, n_sub * pp + 1
            sel_a = head_lanes[0]
            w2 = jnp.where(sel_a, wus[h0][:, :LANES], wus[h1][:, :LANES])
            u0 = jnp.where(sel_a, wus[h0][:, LANES:], wus[h1][:, LANES:])
            q2 = r_ts[pp] + jnp.where(sel_a, zqs[h0][:, :LANES], zqs[h1][:, :LANES])
            y0 = jnp.where(sel_a, zqs[h0][:, LANES:] + yks[h0], zqs[h1][:, LANES:] + yks[h1])
            h16 = h_scr[bi, pp].astype(BF16)
            ys.append(_dot(q2.astype(BF16), h16) + y0)
            top = jnp.concatenate([w2, u0], axis=1).astype(BF16)
            bot = jnp.concatenate([jnp.zeros((c_len, LANES), BF16), v16s[pp]], axis=1)
            gjs.append(_dot_tn(bks[pp], jnp.concatenate([top, bot], axis=0)))

        for pp in pairs:
            gj = gjs[pp]
            g_mat = jnp.where(eye128, jnp.exp(p_ends[pp]), 0.0) + jnp.where(same_head, gj[:, :LANES], 0.0)
            h16 = h_scr[bi, pp].astype(BF16)
            h_scr[bi, pp] = _dot(g_mat.astype(BF16), h16) + jnp.where(same_head, gj[:, LANES:], 0.0)

        means = [_head_sum(y, ones, parts=2) * (1.0 / HEAD_DIM) for y in ys]
        ycs = [y - m for y, m in zip(ys, means)]
        variances = [_head_sum(yc * yc, ones) * (1.0 / HEAD_DIM) for yc in ycs]
        bonuses = [_head_sum(r_ref[bi, rows, cols[pp]] * k_ref[bi, rows, cols[pp]] * rk_ref[:, cols[pp]], ones)
                   for pp in pairs]
        for pp in pairs:
            cs = cols[pp]
            yn = ycs[pp] * lax.rsqrt(variances[pp] + GN_EPS) * lng_ref[:, cs] + lnb_ref[:, cs]
            y_ref[bi, rows, cs] = (yn + bonuses[pp] * v_ref[bi, rows, cs]) * g_ref[bi, rows, cs]

    def chunk_step(ci, carry):
        r0 = pl.multiple_of(ci * c_len, c_len)
        for bi in range(n_bat):
            chunk_group(bi, r0)
        return carry

    lax.fori_loop(0, n_chunks, chunk_step, 0)

    @pl.when(ti == pl.num_programs(1) - 1)
    def _():
        hout_ref[...] = h_scr[...]


def _rwkv_chunk(r, lw, kf, v, nkk, bb, g, h0, r_k, ln_g, ln_b):
    b, t, w = r.shape
    pairs = w // LANES
    tt = min(2 * CHUNK, t)
    n_bat = 2 if b % 2 == 0 else 1
    assert t % tt == 0 and tt % CHUNK == 0 and b % n_bat == 0
    seq = pl.BlockSpec((n_bat, tt, w), lambda bi, ti: (bi, ti, 0))
    par = pl.BlockSpec((1, w), lambda bi, ti: (0, 0))
    st = pl.BlockSpec((n_bat, pairs, LANES, LANES), lambda bi, ti: (bi, 0, 0, 0))
    return pl.pallas_call(
        functools.partial(_rwkv_chunk_kernel, n_bat=n_bat, n_pairs=pairs, n_chunks=tt // CHUNK),
        grid=(b // n_bat, t // tt),
        in_specs=[seq] * 7 + [st, par, par, par],
        out_specs=[seq, st],
        out_shape=[jax.ShapeDtypeStruct((b, t, w), F32),
                   jax.ShapeDtypeStruct((b, pairs, LANES, LANES), F32)],
        scratch_shapes=[pltpu.VMEM((n_bat, pairs, LANES, LANES), F32)],
        compiler_params=_cparams(("parallel", "arbitrary")),
        name="rwkv_chunk",
    )(r, lw, kf, v, nkk, bb, g, h0, r_k, ln_g, ln_b)


def _state_to_blockdiag(s):
    b, h, dv, dk = s.shape
    st = jnp.swapaxes(s, -1, -2).reshape(b, h // 2, 2, dk, dv)
    z = jnp.zeros_like(st[:, :, 0])
    top = jnp.concatenate([st[:, :, 0], z], axis=-1)
    bot = jnp.concatenate([z, st[:, :, 1]], axis=-1)
    return jnp.concatenate([top, bot], axis=-2)


def _blockdiag_to_state(hm):
    b, p, _, _ = hm.shape
    a = hm[:, :, :HEAD_DIM, :HEAD_DIM]
    c = hm[:, :, HEAD_DIM:, HEAD_DIM:]
    st = jnp.stack([a, c], axis=2).reshape(b, 2 * p, HEAD_DIM, HEAD_DIM)
    return jnp.swapaxes(st, -1, -2)


def _out_proj_kernel(x_ref, yr_ref, o_ref, w_ref, g_ref, rw_ref, rb_ref,
                     x1_ref, h2_ref, e_ref, gate_ref):
    half = yr_ref.shape[1]
    x1 = (x_ref[...] + _dot(yr_ref[...].astype(BF16), w_ref[:half, :])
          + _dot(o_ref[...].astype(BF16), w_ref[half:, :]))
    x1_ref[...] = x1
    ms = jnp.mean(x1 * x1, axis=-1, keepdims=True)
    h2 = x1 * lax.rsqrt(ms + RMS_EPS) * g_ref[...]
    h2_ref[...] = h2
    logits = jnp.dot(h2, rw_ref[...], preferred_element_type=F32,
                     precision=lax.Precision.HIGHEST) + rb_ref[...]
    lane = lax.broadcasted_iota(jnp.int32, logits.shape, 1)
    lane_f = lane.astype(F32)
    experts = jnp.zeros(logits.shape, F32)
    tops = jnp.zeros(logits.shape, F32)
    work = logits
    top0 = None
    for kth in range(TOP_K):
        m = jnp.max(work, axis=-1, keepdims=True)
        idx = jnp.min(jnp.where(work == m, lane_f, float(LANES)), axis=-1, keepdims=True)
        if top0 is None:
            top0 = m
        experts = jnp.where(lane == kth, idx, experts)
        tops = jnp.where(lane == kth, jnp.exp(m - top0), tops)
        work = jnp.where(lane_f == idx, -jnp.inf, work)
    e_ref[...] = experts.astype(jnp.int32)
    gate_ref[...] = tops / jnp.sum(tops, axis=-1, keepdims=True)


def _out_proj(x2, yr, o, w_out, norm_g, rw, rb):
    n, d = x2.shape
    tm = min(256, n)
    assert n % tm == 0
    half = yr.shape[1]
    row = lambda i: (i, 0)
    fixed = lambda i: (0, 0)
    return pl.pallas_call(
        _out_proj_kernel,
        grid=(n // tm,),
        in_specs=[
            pl.BlockSpec((tm, d), row),
            pl.BlockSpec((tm, half), row),
            pl.BlockSpec((tm, half), row),
            pl.BlockSpec(w_out.shape, fixed),
            pl.BlockSpec((1, d), fixed),
            pl.BlockSpec(rw.shape, fixed),
            pl.BlockSpec((1, LANES), fixed),
        ],
        out_specs=[
            pl.BlockSpec((tm, d), row),
            pl.BlockSpec((tm, d), row),
            pl.BlockSpec((tm, LANES), row),
            pl.BlockSpec((tm, LANES), row),
        ],
        out_shape=[
            jax.ShapeDtypeStruct((n, d), F32),
            jax.ShapeDtypeStruct((n, d), F32),
            jax.ShapeDtypeStruct((n, LANES), jnp.int32),
            jax.ShapeDtypeStruct((n, LANES), F32),
        ],
        compiler_params=_cparams(("parallel",)),
        name="out_proj",
    )(x2, yr, o, w_out, norm_g, rw, rb)


def _dispatch_kernel(nv_ref, dest_ref, h2_ref, w1_ref, w2_ref, xbuf_hbm, w1b_ref, w2b_ref, zero_scr, sem,
                     *, tokens, n_blocks):
    w1b_ref[...] = w1_ref[...].astype(BF16)
    w2b_ref[...] = w2_ref[...].astype(BF16)

    @pl.when(pl.program_id(0) == 0)
    def _():
        zero_scr[...] = jnp.zeros_like(zero_scr)

        def fill(i):
            r0 = pl.multiple_of(i * MOE_BLOCK, MOE_BLOCK)
            return pltpu.make_async_copy(zero_scr, xbuf_hbm.at[pl.ds(r0, MOE_BLOCK)], sem)

        def fill_start(i, c):
            @pl.when(nv_ref[i] < MOE_BLOCK)
            def _():
                fill(i).start()
            return c

        def fill_wait(i, c):
            @pl.when(nv_ref[i] < MOE_BLOCK)
            def _():
                fill(i).wait()
            return c

        lax.fori_loop(0, n_blocks, fill_start, 0)
        lax.fori_loop(0, n_blocks, fill_wait, 0)

    def copy(a):
        return pltpu.make_async_copy(h2_ref.at[pl.ds(a // TOP_K, 1)],
                                     xbuf_hbm.at[pl.ds(dest_ref[0, 0, a], 1)], sem)

    def start(a, c):
        copy(a).start()
        return c

    def wait(a, c):
        copy(a).wait()
        return c

    lax.fori_loop(0, tokens * TOP_K, start, 0, unroll=8)
    lax.fori_loop(0, tokens * TOP_K, wait, 0, unroll=8)


def _dispatch(h2, dest, n_valid, w1, w2):
    n, d = h2.shape
    steps = 1
    while steps < 128 and n % (2 * steps * SUBLANES) == 0:
        steps *= 2
    tokens = n // steps
    rows = tokens * TOP_K
    n_blocks = n_valid.shape[0]
    w1f = w1.reshape(-1, w1.shape[-1])
    w2f = w2.reshape(-1, w2.shape[-1])
    r1, r2 = w1f.shape[0] // steps, w2f.shape[0] // steps
    assert w1f.shape[0] % steps == 0 and w2f.shape[0] % steps == 0 and r1 % 16 == 0 and r2 % 16 == 0
    grid_spec = pltpu.PrefetchScalarGridSpec(
        num_scalar_prefetch=1,
        grid=(steps,),
        in_specs=[
            pl.BlockSpec((1, 1, rows), lambda i, nv: (i, 0, 0), memory_space=pltpu.SMEM),
            pl.BlockSpec((tokens, d), lambda i, nv: (i, 0)),
            pl.BlockSpec((r1, w1f.shape[1]), lambda i, nv: (i, 0)),
            pl.BlockSpec((r2, w2f.shape[1]), lambda i, nv: (i, 0)),
        ],
        out_specs=[
            pl.BlockSpec(memory_space=pl.ANY),
            pl.BlockSpec((r1, w1f.shape[1]), lambda i, nv: (i, 0)),
            pl.BlockSpec((r2, w2f.shape[1]), lambda i, nv: (i, 0)),
        ],
        scratch_shapes=[pltpu.VMEM((MOE_BLOCK, d), F32), pltpu.SemaphoreType.DMA],
    )
    xbuf, w1b, w2b = pl.pallas_call(
        functools.partial(_dispatch_kernel, tokens=tokens, n_blocks=n_blocks),
        grid_spec=grid_spec,
        out_shape=[jax.ShapeDtypeStruct((n_blocks * MOE_BLOCK, d), F32),
                   jax.ShapeDtypeStruct(w1f.shape, BF16),
                   jax.ShapeDtypeStruct(w2f.shape, BF16)],
        compiler_params=_cparams(("arbitrary",)),
        name="moe_dispatch",
    )(n_valid, dest.reshape(steps, 1, rows), h2, w1f, w2f)
    return xbuf, w1b.reshape(w1.shape), w2b.reshape(w2.shape)


def _moe_ffn_kernel(be_ref, nv_ref, nu_ref, x_ref, w1g_ref, w1l_ref, b1g_ref, b1l_ref, w2_ref, b2_ref,
                    y_ref, xs_scr):
    i = pl.program_id(0)
    f = pl.program_id(1)
    n_valid = nv_ref[i]

    @pl.when((n_valid == 0) & (f == 0))
    def _():
        y_ref[...] = jnp.zeros_like(y_ref)

    @pl.when(n_valid > 0)
    def _():
        @pl.when(f == 0)
        def _():
            xs_scr[...] = x_ref[...].astype(BF16)

        xs = xs_scr[...]
        hg = _dot(xs, w1g_ref[0]) + b1g_ref[0]
        hl = _dot(xs, w1l_ref[0]) + b1l_ref[0]
        glu = jnp.minimum(hg, SWIGLU_LIMIT)
        lin = jnp.clip(hl, -SWIGLU_LIMIT, SWIGLU_LIMIT)
        act = glu * _sigmoid(SWIGLU_ALPHA * glu) * (lin + 1.0)
        part = _dot(act.astype(BF16), w2_ref[0])

        @pl.when(f == 0)
        def _():
            y_ref[...] = part + b2_ref[0]

        @pl.when(f > 0)
        def _():
            y_ref[...] += part


def _moe_ffn(xbuf, block_e, n_valid, n_used, w1, b1, w2, b2):
    n_rows, d = xbuf.shape
    n_exp, _, ff2 = w1.shape
    ff = ff2 // 2
    bm, tf = MOE_BLOCK, MOE_FF_TILE
    n_f = ff // tf
    n_blocks = n_rows // bm
    b1r = b1.reshape(n_exp, 1, ff2)
    b2r = b2.reshape(n_exp, 1, d)
    def ff(i, f, nv):
        return jnp.where(nv[i] > 0, f, n_f - 1)

    def xi(i, nv, nu):
        return jnp.minimum(i, nu[0] - 1)

    grid_spec = pltpu.PrefetchScalarGridSpec(
        num_scalar_prefetch=3,
        grid=(n_blocks, n_f),
        in_specs=[
            pl.BlockSpec((bm, d), lambda i, f, be, nv, nu: (xi(i, nv, nu), 0)),
            pl.BlockSpec((1, d, tf), lambda i, f, be, nv, nu: (be[i], 0, ff(i, f, nv))),
            pl.BlockSpec((1, d, tf), lambda i, f, be, nv, nu: (be[i], 0, n_f + ff(i, f, nv))),
            pl.BlockSpec((1, 1, tf), lambda i, f, be, nv, nu: (be[i], 0, ff(i, f, nv))),
            pl.BlockSpec((1, 1, tf), lambda i, f, be, nv, nu: (be[i], 0, n_f + ff(i, f, nv))),
            pl.BlockSpec((1, tf, d), lambda i, f, be, nv, nu: (be[i], ff(i, f, nv), 0)),
            pl.BlockSpec((1, 1, d), lambda i, f, be, nv, nu: (be[i], 0, 0)),
        ],
        out_specs=pl.BlockSpec((bm, d), lambda i, f, be, nv, nu: (i, 0)),
        scratch_shapes=[pltpu.VMEM((bm, d), BF16)],
    )
    return pl.pallas_call(
        _moe_ffn_kernel,
        grid_spec=grid_spec,
        out_shape=jax.ShapeDtypeStruct((n_rows, d), F32),
        compiler_params=_cparams(("parallel", "arbitrary")),
        name="moe_ffn",
    )(block_e, n_valid, n_used, xbuf, w1, w1, b1r, b1r, w2, b2r)


def _combine_kernel(dest_ref, next_ref, x1_ref, gate_ref, ybuf_hbm, out_ref, rows_scr, sems, *, tokens):
    i = pl.program_id(0)
    n = pl.num_programs(0)
    slot = i % 2

    def copy(idx_ref, buf, a):
        return pltpu.make_async_copy(ybuf_hbm.at[pl.ds(idx_ref[0, 0, a], 1)],
                                     rows_scr.at[buf, a % TOP_K, pl.ds(a // TOP_K, 1)], sems.at[buf])

    def start_all(idx_ref, buf):
        def body(a, c):
            copy(idx_ref, buf, a).start()
            return c
        lax.fori_loop(0, tokens * TOP_K, body, 0, unroll=8)

    @pl.when(i == 0)
    def _():
        start_all(dest_ref, 0)

    @pl.when(i + 1 < n)
    def _():
        start_all(next_ref, 1 - slot)

    def wait(a, c):
        copy(dest_ref, slot, a).wait()
        return c

    lax.fori_loop(0, tokens * TOP_K, wait, 0, unroll=8)
    acc = x1_ref[...]
    gate = gate_ref[...]
    for kth in range(TOP_K):
        acc = acc + gate[:, kth:kth + 1] * rows_scr[slot, kth]
    out_ref[...] = acc


def _combine(x1, gate, dest, ybuf):
    n, d = x1.shape
    tokens = min(GATHER_TOKENS, n)
    assert n % tokens == 0
    steps = n // tokens
    idx = dest.reshape(steps, 1, tokens * TOP_K)
    idx_block = (1, 1, tokens * TOP_K)
    return pl.pallas_call(
        functools.partial(_combine_kernel, tokens=tokens),
        grid=(steps,),
        in_specs=[
            pl.BlockSpec(idx_block, lambda i: (i, 0, 0), memory_space=pltpu.SMEM),
            pl.BlockSpec(idx_block, lambda i: (jnp.minimum(i + 1, steps - 1), 0, 0), memory_space=pltpu.SMEM),
            pl.BlockSpec((tokens, d), lambda i: (i, 0)),
            pl.BlockSpec((tokens, LANES), lambda i: (i, 0)),
            pl.BlockSpec(memory_space=pl.ANY),
        ],
        out_specs=pl.BlockSpec((tokens, d), lambda i: (i, 0)),
        out_shape=jax.ShapeDtypeStruct((n, d), F32),
        scratch_shapes=[pltpu.VMEM((2, TOP_K, tokens, d), F32), pltpu.SemaphoreType.DMA((2,))],
        compiler_params=_cparams(("arbitrary",)),
        name="moe_combine",
    )(idx, idx, x1, gate, ybuf)


def _route(experts, n_exp, n_blocks):
    flat_e = experts.reshape(-1)
    onehot = (flat_e[:, None] == jnp.arange(n_exp, dtype=jnp.int32)[None, :]).astype(jnp.int32)
    csum = jnp.cumsum(onehot, axis=0)
    counts = csum[-1]
    padded = (counts + MOE_BLOCK - 1) // MOE_BLOCK * MOE_BLOCK
    pend = jnp.cumsum(padded)
    pstart = pend - padded
    dest = jnp.sum(onehot * (csum - 1 + pstart[None, :]), axis=1)
    block_start = jnp.arange(n_blocks, dtype=jnp.int32) * MOE_BLOCK
    block_e = jnp.minimum(jnp.sum(block_start[:, None] >= pend[None, :], axis=1), n_exp - 1)
    n_valid = jnp.clip(counts[block_e] - (block_start - pstart[block_e]), 0, MOE_BLOCK)
    n_used = jnp.maximum(pend[-1] // MOE_BLOCK, 1)
    block_e = block_e[jnp.minimum(jnp.arange(n_blocks), n_used - 1)]
    i32 = lambda a: a.astype(jnp.int32)
    return i32(dest), i32(block_e), i32(n_valid), i32(n_used).reshape(1)


def _mixers(x, past_k, past_v, s0, shift_prev, wts):
    b, t, d = x.shape
    width = wts["width"]
    u, q, k, v, k16, v16 = _in_proj(x.reshape(b * t, d), wts["norm1_g"], wts["w_all"], wts["qg"],
                                    wts["kg"], wts["n_u"], wts["n_h"])
    uw = u.shape[1]
    u = u.reshape(b, t, uw)
    q, k, v, k16, v16 = (a.reshape(b, t, width) for a in (q, k, v, k16, v16))
    o = _sb_attend(q, k16, v16, wts["sb_o_g"], past_k, past_v)

    shift_p = jnp.pad(shift_prev, ((0, 0), (0, uw - shift_prev.shape[1]))).reshape(b, 1, uw)
    r, lw, kf, vv, nkk, bb, g = _rwkv_prep(u, shift_p, wts["mu"], wts["w0"], wts["a0"], wts["k_k"],
                                           wts["k_a"], wts["w2p"], wts["a2p"], wts["g2p"], width)
    y_rwkv, h_fin = _rwkv_chunk(r, lw, kf, vv, nkk, bb, g, _state_to_blockdiag(s0),
                                wts["r_k"], wts["ln_g"], wts["ln_b"])
    x1, h2, experts, gates = _out_proj(x.reshape(b * t, d), y_rwkv.reshape(b * t, width),
                                       o.reshape(b * t, width), wts["w_out"], wts["norm2_g"],
                                       wts["rw"], wts["rb"])
    new_k = k.reshape(b, t, width // HEAD_DIM, HEAD_DIM)
    new_v = v.reshape(b, t, width // HEAD_DIM, HEAD_DIM)
    new_shift = u[:, -1, :wts["rwkv_in"]]
    return x1, h2, experts[:, :TOP_K], gates, new_k, new_v, _blockdiag_to_state(h_fin), new_shift


def kernel(x_prompt, x_sample, cache_sb_k, cache_sb_v, state_rwkv, state_shift, norm1_g, w_in, rwkv_mu, rwkv_w0, rwkv_w2, rwkv_a0, rwkv_a2, rwkv_g2, rwkv_k_k, rwkv_k_a, rwkv_r_k, rwkv_ln_g, rwkv_ln_b, sb_q_g, sb_k_g, sb_o_g, w_out, norm2_g, router_w, router_b, moe_w1, moe_b1, moe_w2, moe_b2):
    depth = w_in.shape[0]
    assert depth == 1
    layer = 0
    d = x_prompt.shape[-1]
    width = rwkv_w0.shape[-1]
    rwkv_in = rwkv_mu.shape[-1]
    n_exp = router_w.shape[-1]
    decay_lora, aaa_lora, gate_lora = rwkv_w2.shape[1], rwkv_a2.shape[1], rwkv_g2.shape[1]
    assert decay_lora + aaa_lora == LANES and width % PROJ_TILE == 0
    n_h = width // PROJ_TILE
    n_u = -(-rwkv_in // PROJ_TILE)
    uw = n_u * PROJ_TILE
    tail_w = uw - 3 * width

    w_l = w_in[layer]
    w_all = jnp.concatenate(
        [w_l[:, :rwkv_in], jnp.zeros((d, uw - rwkv_in), F32), w_l[:, rwkv_in:]], axis=1).astype(BF16)
    reps = PROJ_TILE // HEAD_DIM
    row = lambda a: a.reshape(1, -1).astype(F32)
    wts = dict(
        width=width, rwkv_in=rwkv_in, n_u=n_u, n_h=n_h,
        norm1_g=row(norm1_g[layer]), w_all=w_all,
        qg=row(jnp.tile(sb_q_g[layer], reps)), kg=row(jnp.tile(sb_k_g[layer], reps)),
        sb_o_g=row(sb_o_g[layer]),
        mu=jnp.pad(row(rwkv_mu[layer]), ((0, 0), (0, uw - rwkv_in))),
        w0=row(rwkv_w0[layer]), a0=row(rwkv_a0[layer]),
        k_k=row(rwkv_k_k[layer]), k_a=row(rwkv_k_a[layer]),
        w2p=jnp.pad(rwkv_w2[layer], ((0, aaa_lora), (0, 0))).astype(BF16),
        a2p=jnp.pad(rwkv_a2[layer], ((decay_lora, 0), (0, 0))).astype(BF16),
        g2p=jnp.pad(rwkv_g2[layer], ((0, tail_w - LANES - gate_lora), (0, 0))).astype(BF16),
        r_k=row(rwkv_r_k[layer]), ln_g=row(rwkv_ln_g[layer]), ln_b=row(rwkv_ln_b[layer]),
        w_out=w_out[layer].astype(BF16), norm2_g=row(norm2_g[layer]),
        rw=jnp.pad(router_w[layer], ((0, 0), (0, LANES - n_exp))),
        rb=jnp.pad(row(router_b[layer]), ((0, 0), (0, LANES - n_exp)), constant_values=-jnp.inf),
    )

    bp, tp, _ = x_prompt.shape
    bs, ts, _ = x_sample.shape
    heads = width // HEAD_DIM
    s0_prompt = jnp.zeros((bp, heads, HEAD_DIM, HEAD_DIM), F32)
    shift0_prompt = jnp.zeros((bp, rwkv_in), F32)
    past_k = cache_sb_k[layer].reshape(bs, -1, width).astype(BF16)
    past_v = cache_sb_v[layer].reshape(bs, -1, width).astype(BF16)

    grp_p = _mixers(x_prompt, None, None, s0_prompt, shift0_prompt, wts)
    grp_s = _mixers(x_sample, past_k, past_v, state_rwkv[layer], state_shift[layer], wts)

    n_p, n_s = bp * tp, bs * ts
    n_tok = n_p + n_s
    h2 = jnp.concatenate([grp_p[1], grp_s[1]], axis=0)
    experts = jnp.concatenate([grp_p[2], grp_s[2]], axis=0)
    n_blocks = -(-(n_tok * TOP_K) // MOE_BLOCK) + n_exp
    dest, block_e, n_valid, n_used = _route(experts, n_exp, n_blocks)
    xbuf, w1b, w2b = _dispatch(h2, dest, n_valid, moe_w1[layer], moe_w2[layer])
    ybuf = _moe_ffn(xbuf, block_e, n_valid, n_used, w1b, moe_b1[layer], w2b, moe_b2[layer])
    y_p = _combine(grp_p[0], grp_p[3], dest[:n_p * TOP_K], ybuf).reshape(bp, tp, d)
    y_s = _combine(grp_s[0], grp_s[3], dest[n_p * TOP_K:], ybuf).reshape(bs, ts, d)

    lead = lambda a: a[None]
    return (y_p, y_s,
            lead(grp_p[4]), lead(grp_p[5]), lead(grp_p[6]), lead(grp_p[7]),
            lead(grp_s[4]), lead(grp_s[5]), lead(grp_s[6]), lead(grp_s[7]))
```

```python
import functools

import jax
import jax.numpy as jnp
from jax import lax
from jax.experimental import pallas as pl
from jax.experimental.pallas import tpu as pltpu

F32 = jnp.float32
BF16 = jnp.bfloat16

LANES = 128
SUBLANES = 8
VMEM_LIMIT_BYTES = 56 * 1024 * 1024

HEAD_DIM = 64
HEADS_PER_VREG = LANES // HEAD_DIM
TOP_K = 4
RMS_EPS = 1e-5
GN_EPS = 64e-5
SWIGLU_ALPHA = 1.702
SWIGLU_LIMIT = 7.0
KK_NORM_FLOOR = 1e-12

PROJ_TILE = 512
ROW_TILE = 1024
CHUNK = 64
SB_BLOCK = 128
SB_DEAD_LOGIT = -104.0
MOE_BLOCK = 512
MOE_FF_TILE = 512
MOE_ROW_SPLIT = 2
GATHER_TOKENS = 64


def _cparams(sem):
    return pltpu.CompilerParams(dimension_semantics=sem, vmem_limit_bytes=VMEM_LIMIT_BYTES)


def _sigmoid(x):
    return 1.0 / (1.0 + jnp.exp(-x))


def _softplus(x):
    return jnp.maximum(x, 0.0) + jnp.log(1.0 + jnp.exp(-jnp.abs(x)))


def _split_bf16(x, parts):
    out = []
    rem = x
    for _ in range(parts):
        p = rem.astype(BF16)
        out.append(p)
        rem = rem - p.astype(F32)
    return out


def _dot(a, b):
    return jnp.dot(a, b, preferred_element_type=F32)


def _dot_nt(a, b):
    return lax.dot_general(a, b, (((1,), (1,)), ((), ())), preferred_element_type=F32)


def _dot_tn(a, b):
    return lax.dot_general(a, b, (((0,), (0,)), ((), ())), preferred_element_type=F32)


def _head_ones(n=LANES):
    r = lax.broadcasted_iota(jnp.int32, (n, n), 0) // HEAD_DIM
    c = lax.broadcasted_iota(jnp.int32, (n, n), 1) // HEAD_DIM
    return (r == c).astype(BF16)


def _head_sum(x, ones, parts=1):
    acc = None
    for p in _split_bf16(x, parts):
        t = _dot(p, ones)
        acc = t if acc is None else acc + t
    return acc


def _in_proj_kernel(x_ref, g_ref, w_ref, qg_ref, kg_ref, u_ref, q_ref, k_ref, v_ref, k16_ref, v16_ref,
                    h_scr, *, n_u, n_h):
    j = pl.program_id(1)

    @pl.when(j == 0)
    def _():
        x = x_ref[...]
        ms = jnp.mean(x * x, axis=-1, keepdims=True)
        h_scr[...] = (x * lax.rsqrt(ms + RMS_EPS) * g_ref[...]).astype(BF16)

    acc = _dot(h_scr[...], w_ref[...])

    def head_norm(y, gain):
        ones = _head_ones()
        cols = []
        for c in range(y.shape[1] // LANES):
            yc = y[:, c * LANES:(c + 1) * LANES]
            ms = _head_sum(yc * yc, ones) * (1.0 / HEAD_DIM)
            cols.append(yc * lax.rsqrt(ms + RMS_EPS))
        return jnp.concatenate(cols, axis=1) * gain

    @pl.when(j < n_u)
    def _():
        u_ref[...] = acc

    @pl.when((j >= n_u) & (j < n_u + n_h))
    def _():
        q_ref[...] = head_norm(acc, qg_ref[...])

    @pl.when((j >= n_u + n_h) & (j < n_u + 2 * n_h))
    def _():
        kn = head_norm(acc, kg_ref[...])
        k_ref[...] = kn
        k16_ref[...] = kn.astype(BF16)

    @pl.when(j >= n_u + 2 * n_h)
    def _():
        v_ref[...] = acc
        v16_ref[...] = acc.astype(BF16)


def _in_proj(x2, norm_g, w_all, qg, kg, n_u, n_h):
    n, d = x2.shape
    tm = min(ROW_TILE, n)
    tn = PROJ_TILE
    n_tiles = n_u + 3 * n_h
    assert w_all.shape == (d, n_tiles * tn) and n % tm == 0
    hw = n_h * tn

    def clip_map(lo, cnt):
        return lambda i, j: (i, jnp.clip(j - lo, 0, cnt - 1))

    return pl.pallas_call(
        functools.partial(_in_proj_kernel, n_u=n_u, n_h=n_h),
        grid=(n // tm, n_tiles),
        in_specs=[
            pl.BlockSpec((tm, d), lambda i, j: (i, 0)),
            pl.BlockSpec((1, d), lambda i, j: (0, 0)),
            pl.BlockSpec((d, tn), lambda i, j: (0, j)),
            pl.BlockSpec((1, tn), lambda i, j: (0, 0)),
            pl.BlockSpec((1, tn), lambda i, j: (0, 0)),
        ],
        out_specs=[
            pl.BlockSpec((tm, tn), clip_map(0, n_u)),
            pl.BlockSpec((tm, tn), clip_map(n_u, n_h)),
            pl.BlockSpec((tm, tn), clip_map(n_u + n_h, n_h)),
            pl.BlockSpec((tm, tn), clip_map(n_u + 2 * n_h, n_h)),
            pl.BlockSpec((tm, tn), clip_map(n_u + n_h, n_h)),
            pl.BlockSpec((tm, tn), clip_map(n_u + 2 * n_h, n_h)),
        ],
        out_shape=[
            jax.ShapeDtypeStruct((n, n_u * tn), F32),
            jax.ShapeDtypeStruct((n, hw), F32),
            jax.ShapeDtypeStruct((n, hw), F32),
            jax.ShapeDtypeStruct((n, hw), F32),
            jax.ShapeDtypeStruct((n, hw), BF16),
            jax.ShapeDtypeStruct((n, hw), BF16),
        ],
        scratch_shapes=[pltpu.VMEM((tm, d), BF16)],
        compiler_params=_cparams(("parallel", "arbitrary")),
        name="in_proj",
    )(x2, norm_g, w_all, qg, kg)


def _sb_kernel(*refs, tq, n_past, n_grp):
    if n_past:
        q_ref, k_ref, v_ref, pk_ref, pv_ref, og_ref, o_ref, acc_ref = refs
    else:
        q_ref, k_ref, v_ref, og_ref, o_ref, acc_ref = refs
        pk_ref = pv_ref = None
    qi = pl.program_id(2)
    n_heads = HEADS_PER_VREG * n_grp
    lane = lax.broadcasted_iota(jnp.int32, (1, LANES), 1)
    in_a = lane < HEAD_DIM
    q_heads = []
    for g in range(n_grp):
        q = q_ref[0, :, g * LANES:(g + 1) * LANES] * (HEAD_DIM ** -0.5)
        q_heads.append(jnp.where(in_a, q, 0.0).astype(BF16))
        q_heads.append(jnp.where(in_a, 0.0, q).astype(BF16))
    acc_ref[...] = jnp.zeros_like(acc_ref)

    def visit(k_src, v_src, r0, tk, carries, mask):
        rows = pl.ds(r0, tk)
        s_idx = lax.broadcasted_iota(jnp.int32, (tk, tk), 0)
        j_idx = lax.broadcasted_iota(jnp.int32, (tk, tk), 1)
        suffix = (s_idx >= j_idx).astype(BF16)
        kbs = [k_src[0, rows, g * LANES:(g + 1) * LANES] for g in range(n_grp)]
        zs = [_dot_nt(q_heads[h], kbs[h // HEADS_PER_VREG]) for h in range(n_heads)]
        splits = []
        for z in zs:
            log_1mb = -_softplus(z)
            if mask is not None:
                log_1mb = jnp.where(mask, log_1mb, 0.0)
            splits.append(_split_bf16(log_1mb, 2))
        incls = [_dot(hi, suffix) + _dot(lo, suffix) for hi, lo in splits]
        ps = []
        for z, incl, carry in zip(zs, incls, carries):
            p = jnp.exp(z + incl + carry)
            if mask is not None:
                p = jnp.where(mask, p, 0.0)
            ps.append(p.astype(BF16))
        vbs = [v_src[0, rows, g * LANES:(g + 1) * LANES] for g in range(n_grp)]
        for h in range(n_heads):
            acc_ref[h] += _dot(ps[h], vbs[h // HEADS_PER_VREG])
        return tuple(c + incl[:, 0:1] for c, incl in zip(carries, incls))

    q0 = pl.multiple_of(qi * tq, tq)
    t_idx = lax.broadcasted_iota(jnp.int32, (tq, tq), 0)
    j_idx = lax.broadcasted_iota(jnp.int32, (tq, tq), 1)
    zero = jnp.zeros((tq, 1), F32)
    carries = visit(k_ref, v_ref, q0, tq, (zero,) * n_heads, j_idx < t_idx)

    def alive(carries):
        top = carries[0]
        for c in carries[1:]:
            top = jnp.maximum(top, c)
        return (jnp.max(top) > SB_DEAD_LOGIT).astype(jnp.int32)

    def sweep(kr, vr, n_blocks, carries):
        def cond(st):
            return (st[0] >= 0) & (st[1] > 0)

        def body(st):
            b = st[0]
            r0 = pl.multiple_of(b * SB_BLOCK, SB_BLOCK)
            c = visit(kr, vr, r0, SB_BLOCK, st[2:], None)
            return (b - 1, alive(c)) + c

        st = lax.while_loop(cond, body, (n_blocks - 1, alive(carries)) + carries)
        return st[2:]

    if tq == SB_BLOCK:
        carries = sweep(k_ref, v_ref, qi, carries)
    if n_past:
        carries = sweep(pk_ref, pv_ref, jnp.int32(n_past), carries)

    ones = _head_ones()
    for g in range(n_grp):
        o = jnp.where(in_a, acc_ref[HEADS_PER_VREG * g], acc_ref[HEADS_PER_VREG * g + 1])
        ms = _head_sum(o * o, ones) * (1.0 / HEAD_DIM)
        cs = slice(g * LANES, (g + 1) * LANES)
        o_ref[0, :, cs] = o * lax.rsqrt(ms + RMS_EPS) * og_ref[:, cs]


def _sb_attend(q, k, v, og, past_k=None, past_v=None):
    b, t, w = q.shape
    n_grp = 4
    gw = n_grp * LANES
    assert w % gw == 0
    tq = min(SB_BLOCK, t)
    assert t % tq == 0 and (t == tq or tq == SB_BLOCK)
    n_past = 0
    in_specs = [
        pl.BlockSpec((1, tq, gw), lambda bi, p, qi: (bi, qi, p)),
        pl.BlockSpec((1, t, gw), lambda bi, p, qi: (bi, 0, p)),
        pl.BlockSpec((1, t, gw), lambda bi, p, qi: (bi, 0, p)),
    ]
    args = [q, k, v]
    if past_k is not None:
        tp = past_k.shape[1]
        assert tp % SB_BLOCK == 0 and t == tq
        n_past = tp // SB_BLOCK
        in_specs += [pl.BlockSpec((1, tp, gw), lambda bi, p, qi: (bi, 0, p))] * 2
        args += [past_k, past_v]
    in_specs.append(pl.BlockSpec((1, gw), lambda bi, p, qi: (0, p)))
    args.append(og)
    return pl.pallas_call(
        functools.partial(_sb_kernel, tq=tq, n_past=n_past, n_grp=n_grp),
        grid=(b, w // gw, t // tq),
        in_specs=in_specs,
        out_specs=pl.BlockSpec((1, tq, gw), lambda bi, p, qi: (bi, qi, p)),
        out_shape=jax.ShapeDtypeStruct((b, t, w), F32),
        scratch_shapes=[pltpu.VMEM((HEADS_PER_VREG * n_grp, tq, LANES), F32)],
        compiler_params=_cparams(("parallel", "parallel", "arbitrary")),
        name="sb_attend",
    )(*args)


def _rwkv_prep_kernel(u_ref, sh_ref, mu_ref, w0_ref, a0_ref, kk_ref, ka_ref, w2_ref, a2_ref, g2_ref,
                      r_o, lw_o, kf_o, v_o, nkk_o, b_o, g_o, prev_scr, *, width):
    ti = pl.program_id(1)
    tt = u_ref.shape[1]

    @pl.when(ti == 0)
    def _():
        prev_scr[...] = sh_ref[0]

    row = lax.broadcasted_iota(jnp.int32, (tt, 1), 0)

    def mixed(lo, hi):
        u_raw = u_ref[0, :, lo:hi]
        shifted = jnp.where(row == 0, prev_scr[:, lo:hi], pltpu.roll(u_raw, 1, axis=0))
        return u_raw + (shifted - u_raw) * mu_ref[:, lo:hi]

    r = mixed(0, width)
    k = mixed(width, 2 * width)
    v = mixed(2 * width, 3 * width)
    tail = mixed(3 * width, u_ref.shape[2])
    prev_scr[...] = u_ref[0, tt - 1:tt, :]

    lora_in = tail[:, :LANES]
    dw = _dot(jnp.tanh(lora_in).astype(BF16), w2_ref[...])
    da = _dot(lora_in.astype(BF16), a2_ref[...])
    g = _dot(_sigmoid(tail[:, LANES:]).astype(BF16), g2_ref[...])
    w_log = -_softplus(-(w0_ref[...] + dw)) - 0.5
    a = _sigmoid(a0_ref[...] + da)

    kk = k * kk_ref[...]
    ones = _head_ones()
    cols = []
    for c in range(width // LANES):
        kc = kk[:, c * LANES:(c + 1) * LANES]
        norm = jnp.sqrt(_head_sum(kc * kc, ones))
        cols.append(kc / jnp.maximum(norm, KK_NORM_FLOOR))
    kk = jnp.concatenate(cols, axis=1)

    r_o[0] = r
    lw_o[0] = -jnp.exp(w_log)
    kf_o[0] = k * (1.0 + (a - 1.0) * ka_ref[...])
    v_o[0] = v
    nkk_o[0] = -kk
    b_o[0] = kk * a
    g_o[0] = g


def _rwkv_prep(u, shift_prev, mu, w0, a0, k_k, k_a, w2p, a2p, g2p, width):
    b, t, uw = u.shape
    tt = min(256, t)
    assert t % tt == 0
    row_spec = pl.BlockSpec((1, width), lambda bi, ti: (0, 0))
    out_spec = pl.BlockSpec((1, tt, width), lambda bi, ti: (bi, ti, 0))
    return pl.pallas_call(
        functools.partial(_rwkv_prep_kernel, width=width),
        grid=(b, t // tt),
        in_specs=[
            pl.BlockSpec((1, tt, uw), lambda bi, ti: (bi, ti, 0)),
            pl.BlockSpec((1, 1, uw), lambda bi, ti: (bi, 0, 0)),
            pl.BlockSpec((1, uw), lambda bi, ti: (0, 0)),
            row_spec, row_spec, row_spec, row_spec,
            pl.BlockSpec(w2p.shape, lambda bi, ti: (0, 0)),
            pl.BlockSpec(a2p.shape, lambda bi, ti: (0, 0)),
            pl.BlockSpec(g2p.shape, lambda bi, ti: (0, 0)),
        ],
        out_specs=[out_spec] * 7,
        out_shape=[jax.ShapeDtypeStruct((b, t, width), F32)] * 7,
        scratch_shapes=[pltpu.VMEM((1, uw), F32)],
        compiler_params=_cparams(("parallel", "arbitrary")),
        name="rwkv_prep",
    )(u, shift_prev, mu, w0, a0, k_k, k_a, w2p, a2p, g2p)


def _rwkv_chunk_kernel(r_ref, lw_ref, k_ref, v_ref, nkk_ref, b_ref, g_ref, h0_ref,
                       rk_ref, lng_ref, lnb_ref, y_ref, hout_ref, h_scr, *, n_bat, n_pairs, n_chunks):
    ti = pl.program_id(1)
    c_len = CHUNK

    @pl.when(ti == 0)
    def _():
        h_scr[...] = h0_ref[...]

    lane = lax.broadcasted_iota(jnp.int32, (1, LANES), 1)
    head_lanes = (lane < HEAD_DIM, lane >= HEAD_DIM)
    t_idx = lax.broadcasted_iota(jnp.int32, (c_len, c_len), 0)
    i_idx = lax.broadcasted_iota(jnp.int32, (c_len, c_len), 1)
    strict = i_idx < t_idx
    causal = i_idx <= t_idx
    prefix = causal.astype(BF16)
    eye_c = (i_idx == t_idx).astype(F32)
    r128 = lax.broadcasted_iota(jnp.int32, (LANES, LANES), 0)
    c128 = lax.broadcasted_iota(jnp.int32, (LANES, LANES), 1)
    same_head = (r128 // HEAD_DIM) == (c128 // HEAD_DIM)
    eye128 = r128 == c128
    ones = _head_ones()
    n_sub = HEADS_PER_VREG

    def chunk_group(bi, r0):
        rows = pl.ds(r0, c_len)
        pairs = range(n_pairs)
        cols = [slice(pp * LANES, (pp + 1) * LANES) for pp in pairs]
        heads = [(pp, s) for pp in pairs for s in range(n_sub)]

        cums = []
        for pp in pairs:
            parts = _split_bf16(lw_ref[bi, rows, cols[pp]], 3)
            cums.append(_dot(prefix, parts[0]) + _dot(prefix, parts[1]) + _dot(prefix, parts[2]))

        a_ts, r_ts, b16s, k16s, v16s, bks, p_ends = [], [], [], [], [], [], []
        for pp in pairs:
            cs = cols[pp]
            cum = cums[pp]
            lw = lw_ref[bi, rows, cs]
            kf = k_ref[bi, rows, cs]
            bb = b_ref[bi, rows, cs]
            p_inv = jnp.exp(-cum)
            p_end = cum[c_len - 1:c_len, :]
            to_end = jnp.exp(p_end - cum)
            a_ts.append(nkk_ref[bi, rows, cs] * jnp.exp(cum - lw))
            r_ts.append(r_ref[bi, rows, cs] * jnp.exp(cum))
            b16s.append((bb * p_inv).astype(BF16))
            k16s.append((kf * p_inv).astype(BF16))
            v16s.append(v_ref[bi, rows, cs].astype(BF16))
            bks.append(jnp.concatenate([bb * to_end, kf * to_end], axis=0).astype(BF16))
            p_ends.append(p_end)

        m_bs, m_ks = [], []
        for pp, s in heads:
            sel = head_lanes[s]
            lhs = jnp.concatenate([jnp.where(sel, a_ts[pp], 0.0), jnp.where(sel, r_ts[pp], 0.0)],
                                  axis=0).astype(BF16)
            m_bs.append(_dot_nt(lhs, b16s[pp]))
            m_ks.append(_dot_nt(lhs, k16s[pp]))

        a_rbs = [jnp.where(causal, m[c_len:], 0.0).astype(BF16) for m in m_bs]
        a_rks = [jnp.where(causal, m[c_len:], 0.0).astype(BF16) for m in m_ks]
        avs = [_dot(jnp.where(strict, m[:c_len], 0.0).astype(BF16), v16s[pp])
               for m, (pp, _) in zip(m_ks, heads)]
        n_pows = [jnp.where(strict, m[:c_len], 0.0) for m in m_bs]
        t_invs = [eye_c + n for n in n_pows]
        span = 2
        while span < c_len:
            n16s = [n.astype(BF16) for n in n_pows]
            n_pows = [_dot(n, n) for n in n16s]
            t_invs = [_dot(t.astype(BF16), (eye_c + n).astype(BF16)) for t, n in zip(t_invs, n_pows)]
            span *= 2

        wus = [_dot(t.astype(BF16), jnp.concatenate([a_ts[pp], av], axis=1).astype(BF16))
               for t, av, (pp, _) in zip(t_invs, avs, heads)]
        zqs = [_dot(a, wu.astype(BF16)) for a, wu in zip(a_rbs, wus)]
        yks = [_dot(a, v16s[pp]) for a, (pp, _) in zip(a_rks, heads)]

        ys, gjs = [], []
        for pp in pairs:
            h0, h1 = n_sub * pp, n_sub * pp + 1
            sel_a = head_lanes[0]
            w2 = jnp.where(sel_a, wus[h0][:, :LANES], wus[h1][:, :LANES])
            u0 = jnp.where(sel_a, wus[h0][:, LANES:], wus[h1][:, LANES:])
            q2 = r_ts[pp] + jnp.where(sel_a, zqs[h0][:, :LANES], zqs[h1][:, :LANES])
            y0 = jnp.where(sel_a, zqs[h0][:, LANES:] + yks[h0], zqs[h1][:, LANES:] + yks[h1])
            h16 = h_scr[bi, pp].astype(BF16)
            ys.append(_dot(q2.astype(BF16), h16) + y0)
            top = jnp.concatenate([w2, u0], axis=1).astype(BF16)
            bot = jnp.concatenate([jnp.zeros((c_len, LANES), BF16), v16s[pp]], axis=1)
            gjs.append(_dot_tn(bks[pp], jnp.concatenate([top, bot], axis=0)))

        for pp in pairs:
            gj = gjs[pp]
            g_mat = jnp.where(eye128, jnp.exp(p_ends[pp]), 0.0) + jnp.where(same_head, gj[:, :LANES], 0.0)
            h16 = h_scr[bi, pp].astype(BF16)
            h_scr[bi, pp] = _dot(g_mat.astype(BF16), h16) + jnp.where(same_head, gj[:, LANES:], 0.0)

        means = [_head_sum(y, ones, parts=2) * (1.0 / HEAD_DIM) for y in ys]
        ycs = [y - m for y, m in zip(ys, means)]
        variances = [_head_sum(yc * yc, ones) * (1.0 / HEAD_DIM) for yc in ycs]
        bonuses = [_head_sum(r_ref[bi, rows, cols[pp]] * k_ref[bi, rows, cols[pp]] * rk_ref[:, cols[pp]], ones)
                   for pp in pairs]
        for pp in pairs:
            cs = cols[pp]
            yn = ycs[pp] * lax.rsqrt(variances[pp] + GN_EPS) * lng_ref[:, cs] + lnb_ref[:, cs]
            y_ref[bi, rows, cs] = (yn + bonuses[pp] * v_ref[bi, rows, cs]) * g_ref[bi, rows, cs]

    def chunk_step(ci, carry):
        r0 = pl.multiple_of(ci * c_len, c_len)
        for bi in range(n_bat):
            chunk_group(bi, r0)
        return carry

    lax.fori_loop(0, n_chunks, chunk_step, 0)

    @pl.when(ti == pl.num_programs(1) - 1)
    def _():
        hout_ref[...] = h_scr[...]


def _rwkv_chunk(r, lw, kf, v, nkk, bb, g, h0, r_k, ln_g, ln_b):
    b, t, w = r.shape
    pairs = w // LANES
    tt = min(2 * CHUNK, t)
    n_bat = 2 if b % 2 == 0 else 1
    assert t % tt == 0 and tt % CHUNK == 0 and b % n_bat == 0
    seq = pl.BlockSpec((n_bat, tt, w), lambda bi, ti: (bi, ti, 0))
    par = pl.BlockSpec((1, w), lambda bi, ti: (0, 0))
    st = pl.BlockSpec((n_bat, pairs, LANES, LANES), lambda bi, ti: (bi, 0, 0, 0))
    return pl.pallas_call(
        functools.partial(_rwkv_chunk_kernel, n_bat=n_bat, n_pairs=pairs, n_chunks=tt // CHUNK),
        grid=(b // n_bat, t // tt),
        in_specs=[seq] * 7 + [st, par, par, par],
        out_specs=[seq, st],
        out_shape=[jax.ShapeDtypeStruct((b, t, w), F32),
                   jax.ShapeDtypeStruct((b, pairs, LANES, LANES), F32)],
        scratch_shapes=[pltpu.VMEM((n_bat, pairs, LANES, LANES), F32)],
        compiler_params=_cparams(("parallel", "arbitrary")),
        name="rwkv_chunk",
    )(r, lw, kf, v, nkk, bb, g, h0, r_k, ln_g, ln_b)


def _state_to_blockdiag(s):
    b, h, dv, dk = s.shape
    st = jnp.swapaxes(s, -1, -2).reshape(b, h // 2, 2, dk, dv)
    z = jnp.zeros_like(st[:, :, 0])
    top = jnp.concatenate([st[:, :, 0], z], axis=-1)
    bot = jnp.concatenate([z, st[:, :, 1]], axis=-1)
    return jnp.concatenate([top, bot], axis=-2)


def _blockdiag_to_state(hm):
    b, p, _, _ = hm.shape
    a = hm[:, :, :HEAD_DIM, :HEAD_DIM]
    c = hm[:, :, HEAD_DIM:, HEAD_DIM:]
    st = jnp.stack([a, c], axis=2).reshape(b, 2 * p, HEAD_DIM, HEAD_DIM)
    return jnp.swapaxes(st, -1, -2)


def _out_proj_kernel(x_ref, yr_ref, o_ref, w_ref, g_ref, rw_ref, rb_ref,
                     x1_ref, h2_ref, e_ref, gate_ref):
    half = yr_ref.shape[1]
    x1 = (x_ref[...] + _dot(yr_ref[...].astype(BF16), w_ref[:half, :])
          + _dot(o_ref[...].astype(BF16), w_ref[half:, :]))
    x1_ref[...] = x1
    ms = jnp.mean(x1 * x1, axis=-1, keepdims=True)
    h2 = x1 * lax.rsqrt(ms + RMS_EPS) * g_ref[...]
    h2_ref[...] = h2
    logits = jnp.dot(h2, rw_ref[...], preferred_element_type=F32,
                     precision=lax.Precision.HIGHEST) + rb_ref[...]
    lane = lax.broadcasted_iota(jnp.int32, logits.shape, 1)
    lane_f = lane.astype(F32)
    experts = jnp.zeros(logits.shape, F32)
    tops = jnp.zeros(logits.shape, F32)
    work = logits
    top0 = None
    for kth in range(TOP_K):
        m = jnp.max(work, axis=-1, keepdims=True)
        idx = jnp.min(jnp.where(work == m, lane_f, float(LANES)), axis=-1, keepdims=True)
        if top0 is None:
            top0 = m
        experts = jnp.where(lane == kth, idx, experts)
        tops = jnp.where(lane == kth, jnp.exp(m - top0), tops)
        work = jnp.where(lane_f == idx, -jnp.inf, work)
    e_ref[...] = experts.astype(jnp.int32)
    gate_ref[...] = tops / jnp.sum(tops, axis=-1, keepdims=True)


def _out_proj(x2, yr, o, w_out, norm_g, rw, rb):
    n, d = x2.shape
    tm = min(256, n)
    assert n % tm == 0
    half = yr.shape[1]
    row = lambda i: (i, 0)
    fixed = lambda i: (0, 0)
    return pl.pallas_call(
        _out_proj_kernel,
        grid=(n // tm,),
        in_specs=[
            pl.BlockSpec((tm, d), row),
            pl.BlockSpec((tm, half), row),
            pl.BlockSpec((tm, half), row),
            pl.BlockSpec(w_out.shape, fixed),
            pl.BlockSpec((1, d), fixed),
            pl.BlockSpec(rw.shape, fixed),
            pl.BlockSpec((1, LANES), fixed),
        ],
        out_specs=[
            pl.BlockSpec((tm, d), row),
            pl.BlockSpec((tm, d), row),
            pl.BlockSpec((tm, LANES), row),
            pl.BlockSpec((tm, LANES), row),
        ],
        out_shape=[
            jax.ShapeDtypeStruct((n, d), F32),
            jax.ShapeDtypeStruct((n, d), F32),
            jax.ShapeDtypeStruct((n, LANES), jnp.int32),
            jax.ShapeDtypeStruct((n, LANES), F32),
        ],
        compiler_params=_cparams(("parallel",)),
        name="out_proj",
    )(x2, yr, o, w_out, norm_g, rw, rb)


def _dispatch_kernel(nv_ref, dest_ref, h2_ref, w1_ref, w2_ref, xbuf_hbm, w1b_ref, w2b_ref, zero_scr, sem,
                     *, tokens, n_blocks):
    w1b_ref[...] = w1_ref[...].astype(BF16)
    w2b_ref[...] = w2_ref[...].astype(BF16)

    @pl.when(pl.program_id(0) == 0)
    def _():
        zero_scr[...] = jnp.zeros_like(zero_scr)

        def fill(i):
            r0 = pl.multiple_of(i * MOE_BLOCK, MOE_BLOCK)
            return pltpu.make_async_copy(zero_scr, xbuf_hbm.at[pl.ds(r0, MOE_BLOCK)], sem)

        def fill_start(i, c):
            @pl.when(nv_ref[i] < MOE_BLOCK)
            def _():
                fill(i).start()
            return c

        def fill_wait(i, c):
            @pl.when(nv_ref[i] < MOE_BLOCK)
            def _():
                fill(i).wait()
            return c

        lax.fori_loop(0, n_blocks, fill_start, 0)
        lax.fori_loop(0, n_blocks, fill_wait, 0)

    def copy(a):
        return pltpu.make_async_copy(h2_ref.at[pl.ds(a // TOP_K, 1)],
                                     xbuf_hbm.at[pl.ds(dest_ref[0, 0, a], 1)], sem)

    def start(a, c):
        copy(a).start()
        return c

    def wait(a, c):
        copy(a).wait()
        return c

    lax.fori_loop(0, tokens * TOP_K, start, 0, unroll=8)
    lax.fori_loop(0, tokens * TOP_K, wait, 0, unroll=8)


def _dispatch(h2, dest, n_valid, w1, w2):
    n, d = h2.shape
    steps = 1
    while steps < 128 and n % (2 * steps * SUBLANES) == 0:
        steps *= 2
    tokens = n // steps
    rows = tokens * TOP_K
    n_blocks = n_valid.shape[0]
    w1f = w1.reshape(-1, w1.shape[-1])
    w2f = w2.reshape(-1, w2.shape[-1])
    r1, r2 = w1f.shape[0] // steps, w2f.shape[0] // steps
    assert w1f.shape[0] % steps == 0 and w2f.shape[0] % steps == 0 and r1 % 16 == 0 and r2 % 16 == 0
    grid_spec = pltpu.PrefetchScalarGridSpec(
        num_scalar_prefetch=1,
        grid=(steps,),
        in_specs=[
            pl.BlockSpec((1, 1, rows), lambda i, nv: (i, 0, 0), memory_space=pltpu.SMEM),
            pl.BlockSpec((tokens, d), lambda i, nv: (i, 0)),
            pl.BlockSpec((r1, w1f.shape[1]), lambda i, nv: (i, 0)),
            pl.BlockSpec((r2, w2f.shape[1]), lambda i, nv: (i, 0)),
        ],
        out_specs=[
            pl.BlockSpec(memory_space=pl.ANY),
            pl.BlockSpec((r1, w1f.shape[1]), lambda i, nv: (i, 0)),
            pl.BlockSpec((r2, w2f.shape[1]), lambda i, nv: (i, 0)),
        ],
        scratch_shapes=[pltpu.VMEM((MOE_BLOCK, d), F32), pltpu.SemaphoreType.DMA],
    )
    xbuf, w1b, w2b = pl.pallas_call(
        functools.partial(_dispatch_kernel, tokens=tokens, n_blocks=n_blocks),
        grid_spec=grid_spec,
        out_shape=[jax.ShapeDtypeStruct((n_blocks * MOE_BLOCK, d), F32),
                   jax.ShapeDtypeStruct(w1f.shape, BF16),
                   jax.ShapeDtypeStruct(w2f.shape, BF16)],
        compiler_params=_cparams(("arbitrary",)),
        name="moe_dispatch",
    )(n_valid, dest.reshape(steps, 1, rows), h2, w1f, w2f)
    return xbuf, w1b.reshape(w1.shape), w2b.reshape(w2.shape)


def _moe_ffn_kernel(be_ref, nv_ref, nu_ref, x_ref, w1g_ref, w1l_ref, b1g_ref, b1l_ref, w2_ref, b2_ref,
                    y_ref, xs_scr):
    i = pl.program_id(0)
    f = pl.program_id(1)
    n_valid = nv_ref[i]

    @pl.when((n_valid == 0) & (f == 0))
    def _():
        y_ref[...] = jnp.zeros_like(y_ref)

    @pl.when(n_valid > 0)
    def _():
        @pl.when(f == 0)
        def _():
            xs_scr[...] = x_ref[...].astype(BF16)
            y_ref[...] = jnp.broadcast_to(b2_ref[0], y_ref.shape)

        bm = y_ref.shape[0]
        halves = [pl.ds(s * (bm // MOE_ROW_SPLIT), bm // MOE_ROW_SPLIT) for s in range(MOE_ROW_SPLIT)]
        xs = [xs_scr[rows, :] for rows in halves]
        hgs = [_dot(x, w1g_ref[0]) + b1g_ref[0] for x in xs]
        hls = [_dot(x, w1l_ref[0]) + b1l_ref[0] for x in xs]
        acts = []
        for hg, hl in zip(hgs, hls):
            glu = jnp.minimum(hg, SWIGLU_LIMIT)
            lin = jnp.clip(hl, -SWIGLU_LIMIT, SWIGLU_LIMIT)
            acts.append((glu * _sigmoid(SWIGLU_ALPHA * glu) * (lin + 1.0)).astype(BF16))
        parts = [_dot(act, w2_ref[0]) for act in acts]
        for rows, part in zip(halves, parts):
            y_ref[rows, :] += part


def _moe_ffn(xbuf, block_e, n_valid, n_used, w1, b1, w2, b2):
    n_rows, d = xbuf.shape
    n_exp, _, ff2 = w1.shape
    ff = ff2 // 2
    bm, tf = MOE_BLOCK, MOE_FF_TILE
    n_f = ff // tf
    n_blocks = n_rows // bm
    b1r = b1.reshape(n_exp, 1, ff2)
    b2r = b2.reshape(n_exp, 1, d)
    def ff(i, f, nv):
        return jnp.where(nv[i] > 0, f, n_f - 1)

    def xi(i, nv, nu):
        return jnp.minimum(i, nu[0] - 1)

    grid_spec = pltpu.PrefetchScalarGridSpec(
        num_scalar_prefetch=3,
        grid=(n_blocks, n_f),
        in_specs=[
            pl.BlockSpec((bm, d), lambda i, f, be, nv, nu: (xi(i, nv, nu), 0)),
            pl.BlockSpec((1, d, tf), lambda i, f, be, nv, nu: (be[i], 0, ff(i, f, nv))),
            pl.BlockSpec((1, d, tf), lambda i, f, be, nv, nu: (be[i], 0, n_f + ff(i, f, nv))),
            pl.BlockSpec((1, 1, tf), lambda i, f, be, nv, nu: (be[i], 0, ff(i, f, nv))),
            pl.BlockSpec((1, 1, tf), lambda i, f, be, nv, nu: (be[i], 0, n_f + ff(i, f, nv))),
            pl.BlockSpec((1, tf, d), lambda i, f, be, nv, nu: (be[i], ff(i, f, nv), 0)),
            pl.BlockSpec((1, 1, d), lambda i, f, be, nv, nu: (be[i], 0, 0)),
        ],
        out_specs=pl.BlockSpec((bm, d), lambda i, f, be, nv, nu: (i, 0)),
        scratch_shapes=[pltpu.VMEM((bm, d), BF16)],
    )
    return pl.pallas_call(
        _moe_ffn_kernel,
        grid_spec=grid_spec,
        out_shape=jax.ShapeDtypeStruct((n_rows, d), F32),
        compiler_params=_cparams(("parallel", "arbitrary")),
        name="moe_ffn",
    )(block_e, n_valid, n_used, xbuf, w1, w1, b1r, b1r, w2, b2r)


def _combine_kernel(dest_ref, next_ref, x1_ref, gate_ref, ybuf_hbm, out_ref, rows_scr, sems, *, tokens):
    i = pl.program_id(0)
    n = pl.num_programs(0)
    slot = i % 2

    def copy(idx_ref, buf, a):
        return pltpu.make_async_copy(ybuf_hbm.at[pl.ds(idx_ref[0, 0, a], 1)],
                                     rows_scr.at[buf, a % TOP_K, pl.ds(a // TOP_K, 1)], sems.at[buf])

    def start_all(idx_ref, buf):
        def body(a, c):
            copy(idx_ref, buf, a).start()
            return c
        lax.fori_loop(0, tokens * TOP_K, body, 0, unroll=8)

    @pl.when(i == 0)
    def _():
        start_all(dest_ref, 0)

    @pl.when(i + 1 < n)
    def _():
        start_all(next_ref, 1 - slot)

    def wait(a, c):
        copy(dest_ref, slot, a).wait()
        return c

    lax.fori_loop(0, tokens * TOP_K, wait, 0, unroll=8)
    acc = x1_ref[...]
    gate = gate_ref[...]
    for kth in range(TOP_K):
        acc = acc + gate[:, kth:kth + 1] * rows_scr[slot, kth]
    out_ref[...] = acc


def _combine(x1, gate, dest, ybuf):
    n, d = x1.shape
    tokens = min(GATHER_TOKENS, n)
    assert n % tokens == 0
    steps = n // tokens
    idx = dest.reshape(steps, 1, tokens * TOP_K)
    idx_block = (1, 1, tokens * TOP_K)
    return pl.pallas_call(
        functools.partial(_combine_kernel, tokens=tokens),
        grid=(steps,),
        in_specs=[
            pl.BlockSpec(idx_block, lambda i: (i, 0, 0), memory_space=pltpu.SMEM),
            pl.BlockSpec(idx_block, lambda i: (jnp.minimum(i + 1, steps - 1), 0, 0), memory_space=pltpu.SMEM),
            pl.BlockSpec((tokens, d), lambda i: (i, 0)),
            pl.BlockSpec((tokens, LANES), lambda i: (i, 0)),
            pl.BlockSpec(memory_space=pl.ANY),
        ],
        out_specs=pl.BlockSpec((tokens, d), lambda i: (i, 0)),
        out_shape=jax.ShapeDtypeStruct((n, d), F32),
        scratch_shapes=[pltpu.VMEM((2, TOP_K, tokens, d), F32), pltpu.SemaphoreType.DMA((2,))],
        compiler_params=_cparams(("arbitrary",)),
        name="moe_combine",
    )(idx, idx, x1, gate, ybuf)


def _route(experts, n_exp, n_blocks):
    flat_e = experts.reshape(-1)
    onehot = (flat_e[:, None] == jnp.arange(n_exp, dtype=jnp.int32)[None, :]).astype(jnp.int32)
    csum = jnp.cumsum(onehot, axis=0)
    counts = csum[-1]
    padded = (counts + MOE_BLOCK - 1) // MOE_BLOCK * MOE_BLOCK
    pend = jnp.cumsum(padded)
    pstart = pend - padded
    dest = jnp.sum(onehot * (csum - 1 + pstart[None, :]), axis=1)
    block_start = jnp.arange(n_blocks, dtype=jnp.int32) * MOE_BLOCK
    block_e = jnp.minimum(jnp.sum(block_start[:, None] >= pend[None, :], axis=1), n_exp - 1)
    n_valid = jnp.clip(counts[block_e] - (block_start - pstart[block_e]), 0, MOE_BLOCK)
    n_used = jnp.maximum(pend[-1] // MOE_BLOCK, 1)
    block_e = block_e[jnp.minimum(jnp.arange(n_blocks), n_used - 1)]
    i32 = lambda a: a.astype(jnp.int32)
    return i32(dest), i32(block_e), i32(n_valid), i32(n_used).reshape(1)


def _mixers(x, past_k, past_v, s0, shift_prev, wts):
    b, t, d = x.shape
    width = wts["width"]
    u, q, k, v, k16, v16 = _in_proj(x.reshape(b * t, d), wts["norm1_g"], wts["w_all"], wts["qg"],
                                    wts["kg"], wts["n_u"], wts["n_h"])
    uw = u.shape[1]
    u = u.reshape(b, t, uw)
    q, k, v, k16, v16 = (a.reshape(b, t, width) for a in (q, k, v, k16, v16))
    o = _sb_attend(q, k16, v16, wts["sb_o_g"], past_k, past_v)

    shift_p = jnp.pad(shift_prev, ((0, 0), (0, uw - shift_prev.shape[1]))).reshape(b, 1, uw)
    r, lw, kf, vv, nkk, bb, g = _rwkv_prep(u, shift_p, wts["mu"], wts["w0"], wts["a0"], wts["k_k"],
                                           wts["k_a"], wts["w2p"], wts["a2p"], wts["g2p"], width)
    y_rwkv, h_fin = _rwkv_chunk(r, lw, kf, vv, nkk, bb, g, _state_to_blockdiag(s0),
                                wts["r_k"], wts["ln_g"], wts["ln_b"])
    x1, h2, experts, gates = _out_proj(x.reshape(b * t, d), y_rwkv.reshape(b * t, width),
                                       o.reshape(b * t, width), wts["w_out"], wts["norm2_g"],
                                       wts["rw"], wts["rb"])
    new_k = k.reshape(b, t, width // HEAD_DIM, HEAD_DIM)
    new_v = v.reshape(b, t, width // HEAD_DIM, HEAD_DIM)
    new_shift = u[:, -1, :wts["rwkv_in"]]
    return x1, h2, experts[:, :TOP_K], gates, new_k, new_v, _blockdiag_to_state(h_fin), new_shift


def kernel(x_prompt, x_sample, cache_sb_k, cache_sb_v, state_rwkv, state_shift, norm1_g, w_in, rwkv_mu, rwkv_w0, rwkv_w2, rwkv_a0, rwkv_a2, rwkv_g2, rwkv_k_k, rwkv_k_a, rwkv_r_k, rwkv_ln_g, rwkv_ln_b, sb_q_g, sb_k_g, sb_o_g, w_out, norm2_g, router_w, router_b, moe_w1, moe_b1, moe_w2, moe_b2):
    depth = w_in.shape[0]
    assert depth == 1
    layer = 0
    d = x_prompt.shape[-1]
    width = rwkv_w0.shape[-1]
    rwkv_in = rwkv_mu.shape[-1]
    n_exp = router_w.shape[-1]
    decay_lora, aaa_lora, gate_lora = rwkv_w2.shape[1], rwkv_a2.shape[1], rwkv_g2.shape[1]
    assert decay_lora + aaa_lora == LANES and width % PROJ_TILE == 0
    n_h = width // PROJ_TILE
    n_u = -(-rwkv_in // PROJ_TILE)
    uw = n_u * PROJ_TILE
    tail_w = uw - 3 * width

    w_l = w_in[layer]
    w_all = jnp.concatenate(
        [w_l[:, :rwkv_in], jnp.zeros((d, uw - rwkv_in), F32), w_l[:, rwkv_in:]], axis=1).astype(BF16)
    reps = PROJ_TILE // HEAD_DIM
    row = lambda a: a.reshape(1, -1).astype(F32)
    wts = dict(
        width=width, rwkv_in=rwkv_in, n_u=n_u, n_h=n_h,
        norm1_g=row(norm1_g[layer]), w_all=w_all,
        qg=row(jnp.tile(sb_q_g[layer], reps)), kg=row(jnp.tile(sb_k_g[layer], reps)),
        sb_o_g=row(sb_o_g[layer]),
        mu=jnp.pad(row(rwkv_mu[layer]), ((0, 0), (0, uw - rwkv_in))),
        w0=row(rwkv_w0[layer]), a0=row(rwkv_a0[layer]),
        k_k=row(rwkv_k_k[layer]), k_a=row(rwkv_k_a[layer]),
        w2p=jnp.pad(rwkv_w2[layer], ((0, aaa_lora), (0, 0))).astype(BF16),
        a2p=jnp.pad(rwkv_a2[layer], ((decay_lora, 0), (0, 0))).astype(BF16),
        g2p=jnp.pad(rwkv_g2[layer], ((0, tail_w - LANES - gate_lora), (0, 0))).astype(BF16),
        r_k=row(rwkv_r_k[layer]), ln_g=row(rwkv_ln_g[layer]), ln_b=row(rwkv_ln_b[layer]),
        w_out=w_out[layer].astype(BF16), norm2_g=row(norm2_g[layer]),
        rw=jnp.pad(router_w[layer], ((0, 0), (0, LANES - n_exp))),
        rb=jnp.pad(row(router_b[layer]), ((0, 0), (0, LANES - n_exp)), constant_values=-jnp.inf),
    )

    bp, tp, _ = x_prompt.shape
    bs, ts, _ = x_sample.shape
    heads = width // HEAD_DIM
    s0_prompt = jnp.zeros((bp, heads, HEAD_DIM, HEAD_DIM), F32)
    shift0_prompt = jnp.zeros((bp, rwkv_in), F32)
    past_k = cache_sb_k[layer].astype(BF16).reshape(bs, -1, width)
    past_v = cache_sb_v[layer].astype(BF16).reshape(bs, -1, width)

    grp_p = _mixers(x_prompt, None, None, s0_prompt, shift0_prompt, wts)
    grp_s = _mixers(x_sample, past_k, past_v, state_rwkv[layer], state_shift[layer], wts)

    n_p, n_s = bp * tp, bs * ts
    n_tok = n_p + n_s
    h2 = jnp.concatenate([grp_p[1], grp_s[1]], axis=0)
    experts = jnp.concatenate([grp_p[2], grp_s[2]], axis=0)
    n_blocks = -(-(n_tok * TOP_K) // MOE_BLOCK) + n_exp
    dest, block_e, n_valid, n_used = _route(experts, n_exp, n_blocks)
    xbuf, w1b, w2b = _dispatch(h2, dest, n_valid, moe_w1[layer], moe_w2[layer])
    ybuf = _moe_ffn(xbuf, block_e, n_valid, n_used, w1b, moe_b1[layer], w2b, moe_b2[layer])
    y_p = _combine(grp_p[0], grp_p[3], dest[:n_p * TOP_K], ybuf).reshape(bp, tp, d)
    y_s = _combine(grp_s[0], grp_s[3], dest[n_p * TOP_K:], ybuf).reshape(bs, ts, d)

    lead = lambda a: a[None]
    return (y_p, y_s,
            lead(grp_p[4]), lead(grp_p[5]), lead(grp_p[6]), lead(grp_p[7]),
            lead(grp_s[4]), lead(grp_s[5]), lead(grp_s[6]), lead(grp_s[7]))
```

```python
import functools

import jax
import jax.numpy as jnp
from jax import lax
from jax.experimental import pallas as pl
from jax.experimental.pallas import tpu as pltpu

F32 = jnp.float32
BF16 = jnp.bfloat16

LANES = 128
SUBLANES = 8
VMEM_LIMIT_BYTES = 56 * 1024 * 1024

HEAD_DIM = 64
HEADS_PER_VREG = LANES // HEAD_DIM
TOP_K = 4
RMS_EPS = 1e-5
GN_EPS = 64e-5
SWIGLU_ALPHA = 1.702
SWIGLU_LIMIT = 7.0
KK_NORM_FLOOR = 1e-12

PROJ_TILE = 512
ROW_TILE = 1024
CHUNK = 64
SB_BLOCK = 128
SB_DEAD_LOGIT = -104.0
SB_PAST_WINDOW = 512
MOE_BLOCK = 512
MOE_FF_TILE = 512
MOE_ROW_SPLIT = 2
GATHER_TOKENS = 64


def _cparams(sem):
    return pltpu.CompilerParams(dimension_semantics=sem, vmem_limit_bytes=VMEM_LIMIT_BYTES)


def _sigmoid(x):
    return 1.0 / (1.0 + jnp.exp(-x))


def _softplus(x):
    return jnp.maximum(x, 0.0) + jnp.log(1.0 + jnp.exp(-jnp.abs(x)))


def _split_bf16(x, parts):
    out = []
    rem = x
    for _ in range(parts):
        p = rem.astype(BF16)
        out.append(p)
        rem = rem - p.astype(F32)
    return out


def _dot(a, b):
    return jnp.dot(a, b, preferred_element_type=F32)


def _dot_nt(a, b):
    return lax.dot_general(a, b, (((1,), (1,)), ((), ())), preferred_element_type=F32)


def _dot_tn(a, b):
    return lax.dot_general(a, b, (((0,), (0,)), ((), ())), preferred_element_type=F32)


def _head_ones(n=LANES):
    r = lax.broadcasted_iota(jnp.int32, (n, n), 0) // HEAD_DIM
    c = lax.broadcasted_iota(jnp.int32, (n, n), 1) // HEAD_DIM
    return (r == c).astype(BF16)


def _head_sum(x, ones, parts=1):
    acc = None
    for p in _split_bf16(x, parts):
        t = _dot(p, ones)
        acc = t if acc is None else acc + t
    return acc


def _in_proj_kernel(x_ref, g_ref, w_ref, qg_ref, kg_ref, u_ref, q_ref, k_ref, v_ref, k16_ref, v16_ref,
                    h_scr, *, n_u, n_h):
    j = pl.program_id(1)

    @pl.when(j == 0)
    def _():
        x = x_ref[...]
        ms = jnp.mean(x * x, axis=-1, keepdims=True)
        h_scr[...] = (x * lax.rsqrt(ms + RMS_EPS) * g_ref[...]).astype(BF16)

    acc = _dot(h_scr[...], w_ref[...])

    def head_norm(y, gain):
        ones = _head_ones()
        cols = []
        for c in range(y.shape[1] // LANES):
            yc = y[:, c * LANES:(c + 1) * LANES]
            ms = _head_sum(yc * yc, ones) * (1.0 / HEAD_DIM)
            cols.append(yc * lax.rsqrt(ms + RMS_EPS))
        return jnp.concatenate(cols, axis=1) * gain

    @pl.when(j < n_u)
    def _():
        u_ref[...] = acc

    @pl.when((j >= n_u) & (j < n_u + n_h))
    def _():
        q_ref[...] = head_norm(acc, qg_ref[...])

    @pl.when((j >= n_u + n_h) & (j < n_u + 2 * n_h))
    def _():
        kn = head_norm(acc, kg_ref[...])
        k_ref[...] = kn
        k16_ref[...] = kn.astype(BF16)

    @pl.when(j >= n_u + 2 * n_h)
    def _():
        v_ref[...] = acc
        v16_ref[...] = acc.astype(BF16)


def _in_proj(x2, norm_g, w_all, qg, kg, n_u, n_h):
    n, d = x2.shape
    tm = min(ROW_TILE, n)
    tn = PROJ_TILE
    n_tiles = n_u + 3 * n_h
    assert w_all.shape == (d, n_tiles * tn) and n % tm == 0
    hw = n_h * tn

    def clip_map(lo, cnt):
        return lambda i, j: (i, jnp.clip(j - lo, 0, cnt - 1))

    return pl.pallas_call(
        functools.partial(_in_proj_kernel, n_u=n_u, n_h=n_h),
        grid=(n // tm, n_tiles),
        in_specs=[
            pl.BlockSpec((tm, d), lambda i, j: (i, 0)),
            pl.BlockSpec((1, d), lambda i, j: (0, 0)),
            pl.BlockSpec((d, tn), lambda i, j: (0, j)),
            pl.BlockSpec((1, tn), lambda i, j: (0, 0)),
            pl.BlockSpec((1, tn), lambda i, j: (0, 0)),
        ],
        out_specs=[
            pl.BlockSpec((tm, tn), clip_map(0, n_u)),
            pl.BlockSpec((tm, tn), clip_map(n_u, n_h)),
            pl.BlockSpec((tm, tn), clip_map(n_u + n_h, n_h)),
            pl.BlockSpec((tm, tn), clip_map(n_u + 2 * n_h, n_h)),
            pl.BlockSpec((tm, tn), clip_map(n_u + n_h, n_h)),
            pl.BlockSpec((tm, tn), clip_map(n_u + 2 * n_h, n_h)),
        ],
        out_shape=[
            jax.ShapeDtypeStruct((n, n_u * tn), F32),
            jax.ShapeDtypeStruct((n, hw), F32),
            jax.ShapeDtypeStruct((n, hw), F32),
            jax.ShapeDtypeStruct((n, hw), F32),
            jax.ShapeDtypeStruct((n, hw), BF16),
            jax.ShapeDtypeStruct((n, hw), BF16),
        ],
        scratch_shapes=[pltpu.VMEM((tm, d), BF16)],
        compiler_params=_cparams(("parallel", "arbitrary")),
        name="in_proj",
    )(x2, norm_g, w_all, qg, kg)


def _sb_kernel(*refs, tq, n_past, n_grp):
    if n_past:
        q_ref, k_ref, v_ref, pk_ref, pv_ref, og_ref, o_ref, live_ref, acc_ref = refs
    else:
        q_ref, k_ref, v_ref, og_ref, o_ref, live_ref, acc_ref = refs
        pk_ref = pv_ref = None
    qi = pl.program_id(2)
    n_heads = HEADS_PER_VREG * n_grp
    lane = lax.broadcasted_iota(jnp.int32, (1, LANES), 1)
    in_a = lane < HEAD_DIM
    q_heads = []
    for g in range(n_grp):
        q = q_ref[0, :, g * LANES:(g + 1) * LANES] * (HEAD_DIM ** -0.5)
        q_heads.append(jnp.where(in_a, q, 0.0).astype(BF16))
        q_heads.append(jnp.where(in_a, 0.0, q).astype(BF16))
    acc_ref[...] = jnp.zeros_like(acc_ref)

    def visit(k_src, v_src, r0, tk, carries, mask):
        rows = pl.ds(r0, tk)
        s_idx = lax.broadcasted_iota(jnp.int32, (tk, tk), 0)
        j_idx = lax.broadcasted_iota(jnp.int32, (tk, tk), 1)
        suffix = (s_idx >= j_idx).astype(BF16)
        kbs = [k_src[0, rows, g * LANES:(g + 1) * LANES] for g in range(n_grp)]
        zs = [_dot_nt(q_heads[h], kbs[h // HEADS_PER_VREG]) for h in range(n_heads)]
        splits = []
        for z in zs:
            log_1mb = -_softplus(z)
            if mask is not None:
                log_1mb = jnp.where(mask, log_1mb, 0.0)
            splits.append(_split_bf16(log_1mb, 2))
        incls = [_dot(hi, suffix) + _dot(lo, suffix) for hi, lo in splits]
        ps = []
        for z, incl, carry in zip(zs, incls, carries):
            p = jnp.exp(z + incl + carry)
            if mask is not None:
                p = jnp.where(mask, p, 0.0)
            ps.append(p.astype(BF16))
        vbs = [v_src[0, rows, g * LANES:(g + 1) * LANES] for g in range(n_grp)]
        for h in range(n_heads):
            acc_ref[h] += _dot(ps[h], vbs[h // HEADS_PER_VREG])
        return tuple(c + incl[:, 0:1] for c, incl in zip(carries, incls))

    q0 = pl.multiple_of(qi * tq, tq)
    t_idx = lax.broadcasted_iota(jnp.int32, (tq, tq), 0)
    j_idx = lax.broadcasted_iota(jnp.int32, (tq, tq), 1)
    zero = jnp.zeros((tq, 1), F32)
    carries = visit(k_ref, v_ref, q0, tq, (zero,) * n_heads, j_idx < t_idx)

    def alive(carries):
        top = carries[0]
        for c in carries[1:]:
            top = jnp.maximum(top, c)
        return (jnp.max(top) > SB_DEAD_LOGIT).astype(jnp.int32)

    def sweep(kr, vr, n_blocks, carries):
        def cond(st):
            return (st[0] >= 0) & (st[1] > 0)

        def body(st):
            b = st[0]
            r0 = pl.multiple_of(b * SB_BLOCK, SB_BLOCK)
            c = visit(kr, vr, r0, SB_BLOCK, st[2:], None)
            return (b - 1, alive(c)) + c

        st = lax.while_loop(cond, body, (n_blocks - 1, alive(carries)) + carries)
        return st[2:]

    if tq == SB_BLOCK:
        carries = sweep(k_ref, v_ref, qi, carries)
    if n_past:
        carries = sweep(pk_ref, pv_ref, jnp.int32(n_past), carries)
    live_ref[...] = jnp.broadcast_to(alive(carries), live_ref.shape)

    ones = _head_ones()
    for g in range(n_grp):
        o = jnp.where(in_a, acc_ref[HEADS_PER_VREG * g], acc_ref[HEADS_PER_VREG * g + 1])
        ms = _head_sum(o * o, ones) * (1.0 / HEAD_DIM)
        cs = slice(g * LANES, (g + 1) * LANES)
        o_ref[0, :, cs] = o * lax.rsqrt(ms + RMS_EPS) * og_ref[:, cs]


def _sb_attend(q, k, v, og, past_k=None, past_v=None):
    b, t, w = q.shape
    n_grp = 4
    gw = n_grp * LANES
    assert w % gw == 0
    tq = min(SB_BLOCK, t)
    assert t % tq == 0 and (t == tq or tq == SB_BLOCK)
    n_past = 0
    in_specs = [
        pl.BlockSpec((1, tq, gw), lambda bi, p, qi: (bi, qi, p)),
        pl.BlockSpec((1, t, gw), lambda bi, p, qi: (bi, 0, p)),
        pl.BlockSpec((1, t, gw), lambda bi, p, qi: (bi, 0, p)),
    ]
    args = [q, k, v]
    if past_k is not None:
        tp = past_k.shape[1]
        assert tp % SB_BLOCK == 0 and t == tq
        n_past = tp // SB_BLOCK
        in_specs += [pl.BlockSpec((1, tp, gw), lambda bi, p, qi: (bi, 0, p))] * 2
        args += [past_k, past_v]
    in_specs.append(pl.BlockSpec((1, gw), lambda bi, p, qi: (0, p)))
    args.append(og)
    return pl.pallas_call(
        functools.partial(_sb_kernel, tq=tq, n_past=n_past, n_grp=n_grp),
        grid=(b, w // gw, t // tq),
        in_specs=in_specs,
        out_specs=[pl.BlockSpec((1, tq, gw), lambda bi, p, qi: (bi, qi, p)),
                   pl.BlockSpec((1, 1, 1, SUBLANES, LANES), lambda bi, p, qi: (bi, p, qi, 0, 0))],
        out_shape=[jax.ShapeDtypeStruct((b, t, w), F32),
                   jax.ShapeDtypeStruct((b, w // gw, t // tq, SUBLANES, LANES), jnp.int32)],
        scratch_shapes=[pltpu.VMEM((HEADS_PER_VREG * n_grp, tq, LANES), F32)],
        compiler_params=_cparams(("parallel", "parallel", "arbitrary")),
        name="sb_attend",
    )(*args)


def _rwkv_prep_kernel(u_ref, sh_ref, mu_ref, w0_ref, a0_ref, kk_ref, ka_ref, w2_ref, a2_ref, g2_ref,
                      r_o, lw_o, kf_o, v_o, nkk_o, b_o, g_o, prev_scr, *, width):
    ti = pl.program_id(1)
    tt = u_ref.shape[1]

    @pl.when(ti == 0)
    def _():
        prev_scr[...] = sh_ref[0]

    row = lax.broadcasted_iota(jnp.int32, (tt, 1), 0)

    def mixed(lo, hi):
        u_raw = u_ref[0, :, lo:hi]
        shifted = jnp.where(row == 0, prev_scr[:, lo:hi], pltpu.roll(u_raw, 1, axis=0))
        return u_raw + (shifted - u_raw) * mu_ref[:, lo:hi]

    r = mixed(0, width)
    k = mixed(width, 2 * width)
    v = mixed(2 * width, 3 * width)
    tail = mixed(3 * width, u_ref.shape[2])
    prev_scr[...] = u_ref[0, tt - 1:tt, :]

    lora_in = tail[:, :LANES]
    dw = _dot(jnp.tanh(lora_in).astype(BF16), w2_ref[...])
    da = _dot(lora_in.astype(BF16), a2_ref[...])
    g = _dot(_sigmoid(tail[:, LANES:]).astype(BF16), g2_ref[...])
    w_log = -_softplus(-(w0_ref[...] + dw)) - 0.5
    a = _sigmoid(a0_ref[...] + da)

    kk = k * kk_ref[...]
    ones = _head_ones()
    cols = []
    for c in range(width // LANES):
        kc = kk[:, c * LANES:(c + 1) * LANES]
        norm = jnp.sqrt(_head_sum(kc * kc, ones))
        cols.append(kc / jnp.maximum(norm, KK_NORM_FLOOR))
    kk = jnp.concatenate(cols, axis=1)

    r_o[0] = r
    lw_o[0] = -jnp.exp(w_log)
    kf_o[0] = k * (1.0 + (a - 1.0) * ka_ref[...])
    v_o[0] = v
    nkk_o[0] = -kk
    b_o[0] = kk * a
    g_o[0] = g


def _rwkv_prep(u, shift_prev, mu, w0, a0, k_k, k_a, w2p, a2p, g2p, width):
    b, t, uw = u.shape
    tt = min(256, t)
    assert t % tt == 0
    row_spec = pl.BlockSpec((1, width), lambda bi, ti: (0, 0))
    out_spec = pl.BlockSpec((1, tt, width), lambda bi, ti: (bi, ti, 0))
    return pl.pallas_call(
        functools.partial(_rwkv_prep_kernel, width=width),
        grid=(b, t // tt),
        in_specs=[
            pl.BlockSpec((1, tt, uw), lambda bi, ti: (bi, ti, 0)),
            pl.BlockSpec((1, 1, uw), lambda bi, ti: (bi, 0, 0)),
            pl.BlockSpec((1, uw), lambda bi, ti: (0, 0)),
            row_spec, row_spec, row_spec, row_spec,
            pl.BlockSpec(w2p.shape, lambda bi, ti: (0, 0)),
            pl.BlockSpec(a2p.shape, lambda bi, ti: (0, 0)),
            pl.BlockSpec(g2p.shape, lambda bi, ti: (0, 0)),
        ],
        out_specs=[out_spec] * 7,
        out_shape=[jax.ShapeDtypeStruct((b, t, width), F32)] * 7,
        scratch_shapes=[pltpu.VMEM((1, uw), F32)],
        compiler_params=_cparams(("parallel", "arbitrary")),
        name="rwkv_prep",
    )(u, shift_prev, mu, w0, a0, k_k, k_a, w2p, a2p, g2p)


def _rwkv_chunk_kernel(r_ref, lw_ref, k_ref, v_ref, nkk_ref, b_ref, g_ref, h0_ref,
                       rk_ref, lng_ref, lnb_ref, y_ref, hout_ref, h_scr, *, n_bat, n_pairs, n_chunks):
    ti = pl.program_id(1)
    c_len = CHUNK

    @pl.when(ti == 0)
    def _():
        h_scr[...] = h0_ref[...]

    lane = lax.broadcasted_iota(jnp.int32, (1, LANES), 1)
    head_lanes = (lane < HEAD_DIM, lane >= HEAD_DIM)
    t_idx = lax.broadcasted_iota(jnp.int32, (c_len, c_len), 0)
    i_idx = lax.broadcasted_iota(jnp.int32, (c_len, c_len), 1)
    strict = i_idx < t_idx
    causal = i_idx <= t_idx
    prefix = causal.astype(BF16)
    eye_c = (i_idx == t_idx).astype(F32)
    r128 = lax.broadcasted_iota(jnp.int32, (LANES, LANES), 0)
    c128 = lax.broadcasted_iota(jnp.int32, (LANES, LANES), 1)
    same_head = (r128 // HEAD_DIM) == (c128 // HEAD_DIM)
    eye128 = r128 == c128
    ones = _head_ones()
    n_sub = HEADS_PER_VREG

    def chunk_group(bi, r0):
        rows = pl.ds(r0, c_len)
        pairs = range(n_pairs)
        cols = [slice(pp * LANES, (pp + 1) * LANES) for pp in pairs]
        heads = [(pp, s) for pp in pairs for s in range(n_sub)]

        cums = []
        for pp in pairs:
            parts = _split_bf16(lw_ref[bi, rows, cols[pp]], 3)
            cums.append(_dot(prefix, parts[0]) + _dot(prefix, parts[1]) + _dot(prefix, parts[2]))

        a_ts, r_ts, b16s, k16s, v16s, bks, p_ends = [], [], [], [], [], [], []
        for pp in pairs:
            cs = cols[pp]
            cum = cums[pp]
            lw = lw_ref[bi, rows, cs]
            kf = k_ref[bi, rows, cs]
            bb = b_ref[bi, rows, cs]
            p_inv = jnp.exp(-cum)
            p_end = cum[c_len - 1:c_len, :]
            to_end = jnp.exp(p_end - cum)
            a_ts.append(nkk_ref[bi, rows, cs] * jnp.exp(cum - lw))
            r_ts.append(r_ref[bi, rows, cs] * jnp.exp(cum))
            b16s.append((bb * p_inv).astype(BF16))
            k16s.append((kf * p_inv).astype(BF16))
            v16s.append(v_ref[bi, rows, cs].astype(BF16))
            bks.append(jnp.concatenate([bb * to_end, kf * to_end], axis=0).astype(BF16))
            p_ends.append(p_end)

        m_bs, m_ks = [], []
        for pp, s in heads:
            sel = head_lanes[s]
            lhs = jnp.concatenate([jnp.where(sel, a_ts[pp], 0.0), jnp.where(sel, r_ts[pp], 0.0)],
                                  axis=0).astype(BF16)
            m_bs.append(_dot_nt(lhs, b16s[pp]))
            m_ks.append(_dot_nt(lhs, k16s[pp]))

        a_rbs = [jnp.where(causal, m[c_len:], 0.0).astype(BF16) for m in m_bs]
        a_rks = [jnp.where(causal, m[c_len:], 0.0).astype(BF16) for m in m_ks]
        avs = [_dot(jnp.where(strict, m[:c_len], 0.0).astype(BF16), v16s[pp])
               for m, (pp, _) in zip(m_ks, heads)]
        n_pows = [jnp.where(strict, m[:c_len], 0.0) for m in m_bs]
        t_invs = [eye_c + n for n in n_pows]
        span = 2
        while span < c_len:
            n16s = [n.astype(BF16) for n in n_pows]
            n_pows = [_dot(n, n) for n in n16s]
            t_invs = [_dot(t.astype(BF16), (eye_c + n).astype(BF16)) for t, n in zip(t_invs, n_pows)]
            span *= 2

        wus = [_dot(t.astype(BF16), jnp.concatenate([a_ts[pp], av], axis=1).astype(BF16))
               for t, av, (pp, _) in zip(t_invs, avs, heads)]
        zqs = [_dot(a, wu.astype(BF16)) for a, wu in zip(a_rbs, wus)]
        yks = [_dot(a, v16s[pp]) for a, (pp, _) in zip(a_rks, heads)]

        ys, gjs = [], []
        for pp in pairs:
            h0, h1 = n_sub * pp, n_sub * pp + 1
            sel_a = head_lanes[0]
            w2 = jnp.where(sel_a, wus[h0][:, :LANES], wus[h1][:, :LANES])
            u0 = jnp.where(sel_a, wus[h0][:, LANES:], wus[h1][:, LANES:])
            q2 = r_ts[pp] + jnp.where(sel_a, zqs[h0][:, :LANES], zqs[h1][:, :LANES])
            y0 = jnp.where(sel_a, zqs[h0][:, LANES:] + yks[h0], zqs[h1][:, LANES:] + yks[h1])
            h16 = h_scr[bi, pp].astype(BF16)
            ys.append(_dot(q2.astype(BF16), h16) + y0)
            top = jnp.concatenate([w2, u0], axis=1).astype(BF16)
            bot = jnp.concatenate([jnp.zeros((c_len, LANES), BF16), v16s[pp]], axis=1)
            gjs.append(_dot_tn(bks[pp], jnp.concatenate([top, bot], axis=0)))

        for pp in pairs:
            gj = gjs[pp]
            g_mat = jnp.where(eye128, jnp.exp(p_ends[pp]), 0.0) + jnp.where(same_head, gj[:, :LANES], 0.0)
            h16 = h_scr[bi, pp].astype(BF16)
            h_scr[bi, pp] = _dot(g_mat.astype(BF16), h16) + jnp.where(same_head, gj[:, LANES:], 0.0)

        means = [_head_sum(y, ones, parts=2) * (1.0 / HEAD_DIM) for y in ys]
        ycs = [y - m for y, m in zip(ys, means)]
        variances = [_head_sum(yc * yc, ones) * (1.0 / HEAD_DIM) for yc in ycs]
        bonuses = [_head_sum(r_ref[bi, rows, cols[pp]] * k_ref[bi, rows, cols[pp]] * rk_ref[:, cols[pp]], ones)
                   for pp in pairs]
        for pp in pairs:
            cs = cols[pp]
            yn = ycs[pp] * lax.rsqrt(variances[pp] + GN_EPS) * lng_ref[:, cs] + lnb_ref[:, cs]
            y_ref[bi, rows, cs] = (yn + bonuses[pp] * v_ref[bi, rows, cs]) * g_ref[bi, rows, cs]

    def chunk_step(ci, carry):
        r0 = pl.multiple_of(ci * c_len, c_len)
        for bi in range(n_bat):
            chunk_group(bi, r0)
        return carry

    lax.fori_loop(0, n_chunks, chunk_step, 0)

    @pl.when(ti == pl.num_programs(1) - 1)
    def _():
        hout_ref[...] = h_scr[...]


def _rwkv_chunk(r, lw, kf, v, nkk, bb, g, h0, r_k, ln_g, ln_b):
    b, t, w = r.shape
    pairs = w // LANES
    tt = min(2 * CHUNK, t)
    n_bat = 2 if b % 2 == 0 else 1
    assert t % tt == 0 and tt % CHUNK == 0 and b % n_bat == 0
    seq = pl.BlockSpec((n_bat, tt, w), lambda bi, ti: (bi, ti, 0))
    par = pl.BlockSpec((1, w), lambda bi, ti: (0, 0))
    st = pl.BlockSpec((n_bat, pairs, LANES, LANES), lambda bi, ti: (bi, 0, 0, 0))
    return pl.pallas_call(
        functools.partial(_rwkv_chunk_kernel, n_bat=n_bat, n_pairs=pairs, n_chunks=tt // CHUNK),
        grid=(b // n_bat, t // tt),
        in_specs=[seq] * 7 + [st, par, par, par],
        out_specs=[seq, st],
        out_shape=[jax.ShapeDtypeStruct((b, t, w), F32),
                   jax.ShapeDtypeStruct((b, pairs, LANES, LANES), F32)],
        scratch_shapes=[pltpu.VMEM((n_bat, pairs, LANES, LANES), F32)],
        compiler_params=_cparams(("parallel", "arbitrary")),
        name="rwkv_chunk",
    )(r, lw, kf, v, nkk, bb, g, h0, r_k, ln_g, ln_b)


def _state_to_blockdiag(s):
    b, h, dv, dk = s.shape
    st = jnp.swapaxes(s, -1, -2).reshape(b, h // 2, 2, dk, dv)
    z = jnp.zeros_like(st[:, :, 0])
    top = jnp.concatenate([st[:, :, 0], z], axis=-1)
    bot = jnp.concatenate([z, st[:, :, 1]], axis=-1)
    return jnp.concatenate([top, bot], axis=-2)


def _blockdiag_to_state(hm):
    b, p, _, _ = hm.shape
    a = hm[:, :, :HEAD_DIM, :HEAD_DIM]
    c = hm[:, :, HEAD_DIM:, HEAD_DIM:]
    st = jnp.stack([a, c], axis=2).reshape(b, 2 * p, HEAD_DIM, HEAD_DIM)
    return jnp.swapaxes(st, -1, -2)


def _out_proj_kernel(x_ref, yr_ref, o_ref, w_ref, g_ref, rw_ref, rb_ref,
                     x1_ref, h2_ref, e_ref, gate_ref):
    half = yr_ref.shape[1]
    x1 = (x_ref[...] + _dot(yr_ref[...].astype(BF16), w_ref[:half, :])
          + _dot(o_ref[...].astype(BF16), w_ref[half:, :]))
    x1_ref[...] = x1
    ms = jnp.mean(x1 * x1, axis=-1, keepdims=True)
    h2 = x1 * lax.rsqrt(ms + RMS_EPS) * g_ref[...]
    h2_ref[...] = h2
    h_hi, h_lo = _split_bf16(h2, 2)
    both = _dot(h_hi, rw_ref[...])
    logits = both[:, :LANES] + both[:, LANES:] + _dot(h_lo, rw_ref[:, :LANES]) + rb_ref[...]
    lane = lax.broadcasted_iota(jnp.int32, logits.shape, 1)
    lane_f = lane.astype(F32)
    experts = jnp.zeros(logits.shape, F32)
    tops = jnp.zeros(logits.shape, F32)
    work = logits
    top0 = None
    for kth in range(TOP_K):
        m = jnp.max(work, axis=-1, keepdims=True)
        idx = jnp.min(jnp.where(work == m, lane_f, float(LANES)), axis=-1, keepdims=True)
        if top0 is None:
            top0 = m
        experts = jnp.where(lane == kth, idx, experts)
        tops = jnp.where(lane == kth, jnp.exp(m - top0), tops)
        work = jnp.where(lane_f == idx, -jnp.inf, work)
    e_ref[...] = experts.astype(jnp.int32)
    gate_ref[...] = tops / jnp.sum(tops, axis=-1, keepdims=True)


def _out_proj(x2, yr, o, w_out, norm_g, rw, rb):
    n, d = x2.shape
    tm = min(256, n)
    assert n % tm == 0
    half = yr.shape[1]
    row = lambda i: (i, 0)
    fixed = lambda i: (0, 0)
    return pl.pallas_call(
        _out_proj_kernel,
        grid=(n // tm,),
        in_specs=[
            pl.BlockSpec((tm, d), row),
            pl.BlockSpec((tm, half), row),
            pl.BlockSpec((tm, half), row),
            pl.BlockSpec(w_out.shape, fixed),
            pl.BlockSpec((1, d), fixed),
            pl.BlockSpec(rw.shape, fixed),
            pl.BlockSpec((1, LANES), fixed),
        ],
        out_specs=[
            pl.BlockSpec((tm, d), row),
            pl.BlockSpec((tm, d), row),
            pl.BlockSpec((tm, LANES), row),
            pl.BlockSpec((tm, LANES), row),
        ],
        out_shape=[
            jax.ShapeDtypeStruct((n, d), F32),
            jax.ShapeDtypeStruct((n, d), F32),
            jax.ShapeDtypeStruct((n, LANES), jnp.int32),
            jax.ShapeDtypeStruct((n, LANES), F32),
        ],
        compiler_params=_cparams(("parallel",)),
        name="out_proj",
    )(x2, yr, o, w_out, norm_g, rw, rb)


def _dispatch_kernel(nv_ref, dest_ref, h2_ref, w1_ref, w2_ref, xbuf_hbm, w1b_ref, w2b_ref, zero_scr, sem,
                     *, tokens, n_blocks):
    w1b_ref[...] = w1_ref[...].astype(BF16)
    w2b_ref[...] = w2_ref[...].astype(BF16)

    @pl.when(pl.program_id(0) == 0)
    def _():
        zero_scr[...] = jnp.zeros_like(zero_scr)

        def fill(i):
            r0 = pl.multiple_of(i * MOE_BLOCK, MOE_BLOCK)
            return pltpu.make_async_copy(zero_scr, xbuf_hbm.at[pl.ds(r0, MOE_BLOCK)], sem)

        def fill_start(i, c):
            @pl.when(nv_ref[i] < MOE_BLOCK)
            def _():
                fill(i).start()
            return c

        def fill_wait(i, c):
            @pl.when(nv_ref[i] < MOE_BLOCK)
            def _():
                fill(i).wait()
            return c

        lax.fori_loop(0, n_blocks, fill_start, 0)
        lax.fori_loop(0, n_blocks, fill_wait, 0)

    def copy(t0, t, k):
        return pltpu.make_async_copy(h2_ref.at[pl.ds(t0 + t, 1)],
                                     xbuf_hbm.at[pl.ds(dest_ref[0, 0, (t0 + t) * TOP_K + k], 1)], sem)

    def for_rows(fn):
        def group(g, c):
            t0 = pl.multiple_of(g * SUBLANES, SUBLANES)
            for t in range(SUBLANES):
                for k in range(TOP_K):
                    fn(copy(t0, t, k))
            return c
        lax.fori_loop(0, tokens // SUBLANES, group, 0)

    for_rows(lambda c: c.start())
    for_rows(lambda c: c.wait())


def _dispatch(h2, dest, n_valid, w1, w2):
    n, d = h2.shape
    steps = 1
    while steps < 128 and n % (2 * steps * SUBLANES) == 0:
        steps *= 2
    tokens = n // steps
    rows = tokens * TOP_K
    n_blocks = n_valid.shape[0]
    w1f = w1.reshape(-1, w1.shape[-1])
    w2f = w2.reshape(-1, w2.shape[-1])
    r1, r2 = w1f.shape[0] // steps, w2f.shape[0] // steps
    assert w1f.shape[0] % steps == 0 and w2f.shape[0] % steps == 0 and r1 % 16 == 0 and r2 % 16 == 0
    grid_spec = pltpu.PrefetchScalarGridSpec(
        num_scalar_prefetch=1,
        grid=(steps,),
        in_specs=[
            pl.BlockSpec((1, 1, rows), lambda i, nv: (i, 0, 0), memory_space=pltpu.SMEM),
            pl.BlockSpec((tokens, d), lambda i, nv: (i, 0)),
            pl.BlockSpec((r1, w1f.shape[1]), lambda i, nv: (i, 0)),
            pl.BlockSpec((r2, w2f.shape[1]), lambda i, nv: (i, 0)),
        ],
        out_specs=[
            pl.BlockSpec(memory_space=pl.ANY),
            pl.BlockSpec((r1, w1f.shape[1]), lambda i, nv: (i, 0)),
            pl.BlockSpec((r2, w2f.shape[1]), lambda i, nv: (i, 0)),
        ],
        scratch_shapes=[pltpu.VMEM((MOE_BLOCK, d), F32), pltpu.SemaphoreType.DMA],
    )
    xbuf, w1b, w2b = pl.pallas_call(
        functools.partial(_dispatch_kernel, tokens=tokens, n_blocks=n_blocks),
        grid_spec=grid_spec,
        out_shape=[jax.ShapeDtypeStruct((n_blocks * MOE_BLOCK, d), F32),
                   jax.ShapeDtypeStruct(w1f.shape, BF16),
                   jax.ShapeDtypeStruct(w2f.shape, BF16)],
        compiler_params=_cparams(("arbitrary",)),
        name="moe_dispatch",
    )(n_valid, dest.reshape(steps, 1, rows), h2, w1f, w2f)
    return xbuf, w1b.reshape(w1.shape), w2b.reshape(w2.shape)


def _moe_ffn_kernel(be_ref, nv_ref, nu_ref, x_ref, w1g_ref, w1l_ref, b1g_ref, b1l_ref, w2_ref, b2_ref,
                    y_ref, xs_scr):
    i = pl.program_id(0)
    f = pl.program_id(1)
    n_valid = nv_ref[i]

    @pl.when((n_valid == 0) & (f == 0))
    def _():
        y_ref[...] = jnp.zeros_like(y_ref)

    @pl.when(n_valid > 0)
    def _():
        @pl.when(f == 0)
        def _():
            xs_scr[...] = x_ref[...].astype(BF16)
            y_ref[...] = jnp.broadcast_to(b2_ref[0], y_ref.shape)

        bm = y_ref.shape[0]
        halves = [pl.ds(s * (bm // MOE_ROW_SPLIT), bm // MOE_ROW_SPLIT) for s in range(MOE_ROW_SPLIT)]
        xs = [xs_scr[rows, :] for rows in halves]
        hgs = [_dot(x, w1g_ref[0]) + b1g_ref[0] for x in xs]
        hls = [_dot(x, w1l_ref[0]) + b1l_ref[0] for x in xs]
        acts = []
        for hg, hl in zip(hgs, hls):
            glu = jnp.minimum(hg, SWIGLU_LIMIT)
            lin = jnp.clip(hl, -SWIGLU_LIMIT, SWIGLU_LIMIT)
            acts.append((glu * _sigmoid(SWIGLU_ALPHA * glu) * (lin + 1.0)).astype(BF16))
        parts = [_dot(act, w2_ref[0]) for act in acts]
        for rows, part in zip(halves, parts):
            y_ref[rows, :] += part


def _moe_ffn(xbuf, block_e, n_valid, n_used, w1, b1, w2, b2):
    n_rows, d = xbuf.shape
    n_exp, _, ff2 = w1.shape
    ff = ff2 // 2
    bm, tf = MOE_BLOCK, MOE_FF_TILE
    n_f = ff // tf
    n_blocks = n_rows // bm
    b1r = b1.reshape(n_exp, 1, ff2)
    b2r = b2.reshape(n_exp, 1, d)
    def ff(i, f, nv):
        return jnp.where(nv[i] > 0, f, n_f - 1)

    def xi(i, nv, nu):
        return jnp.minimum(i, nu[0] - 1)

    grid_spec = pltpu.PrefetchScalarGridSpec(
        num_scalar_prefetch=3,
        grid=(n_blocks, n_f),
        in_specs=[
            pl.BlockSpec((bm, d), lambda i, f, be, nv, nu: (xi(i, nv, nu), 0)),
            pl.BlockSpec((1, d, tf), lambda i, f, be, nv, nu: (be[i], 0, ff(i, f, nv))),
            pl.BlockSpec((1, d, tf), lambda i, f, be, nv, nu: (be[i], 0, n_f + ff(i, f, nv))),
            pl.BlockSpec((1, 1, tf), lambda i, f, be, nv, nu: (be[i], 0, ff(i, f, nv))),
            pl.BlockSpec((1, 1, tf), lambda i, f, be, nv, nu: (be[i], 0, n_f + ff(i, f, nv))),
            pl.BlockSpec((1, tf, d), lambda i, f, be, nv, nu: (be[i], ff(i, f, nv), 0)),
            pl.BlockSpec((1, 1, d), lambda i, f, be, nv, nu: (be[i], 0, 0)),
        ],
        out_specs=pl.BlockSpec((bm, d), lambda i, f, be, nv, nu: (i, 0)),
        scratch_shapes=[pltpu.VMEM((bm, d), BF16)],
    )
    return pl.pallas_call(
        _moe_ffn_kernel,
        grid_spec=grid_spec,
        out_shape=jax.ShapeDtypeStruct((n_rows, d), F32),
        compiler_params=_cparams(("parallel", "arbitrary")),
        name="moe_ffn",
    )(block_e, n_valid, n_used, xbuf, w1, w1, b1r, b1r, w2, b2r)


def _combine_kernel(dest_ref, next_ref, x1_ref, gate_ref, ybuf_hbm, out_ref, rows_scr, sems, *, tokens):
    i = pl.program_id(0)
    n = pl.num_programs(0)
    slot = i % 2

    def copy(idx_ref, buf, t0, t, k):
        return pltpu.make_async_copy(ybuf_hbm.at[pl.ds(idx_ref[0, 0, (t0 + t) * TOP_K + k], 1)],
                                     rows_scr.at[buf, k, pl.ds(t0 + t, 1)], sems.at[buf])

    def for_rows(idx_ref, buf, fn):
        def group(g, c):
            t0 = pl.multiple_of(g * SUBLANES, SUBLANES)
            for t in range(SUBLANES):
                for k in range(TOP_K):
                    fn(copy(idx_ref, buf, t0, t, k))
            return c
        lax.fori_loop(0, tokens // SUBLANES, group, 0)

    def start_all(idx_ref, buf):
        for_rows(idx_ref, buf, lambda c: c.start())

    @pl.when(i == 0)
    def _():
        start_all(dest_ref, 0)

    @pl.when(i + 1 < n)
    def _():
        start_all(next_ref, 1 - slot)

    for_rows(dest_ref, slot, lambda c: c.wait())
    acc = x1_ref[...]
    gate = gate_ref[...]
    for kth in range(TOP_K):
        acc = acc + gate[:, kth:kth + 1] * rows_scr[slot, kth]
    out_ref[...] = acc


def _combine(x1, gate, dest, ybuf):
    n, d = x1.shape
    tokens = min(GATHER_TOKENS, n)
    assert n % tokens == 0
    steps = n // tokens
    idx = dest.reshape(steps, 1, tokens * TOP_K)
    idx_block = (1, 1, tokens * TOP_K)
    return pl.pallas_call(
        functools.partial(_combine_kernel, tokens=tokens),
        grid=(steps,),
        in_specs=[
            pl.BlockSpec(idx_block, lambda i: (i, 0, 0), memory_space=pltpu.SMEM),
            pl.BlockSpec(idx_block, lambda i: (jnp.minimum(i + 1, steps - 1), 0, 0), memory_space=pltpu.SMEM),
            pl.BlockSpec((tokens, d), lambda i: (i, 0)),
            pl.BlockSpec((tokens, LANES), lambda i: (i, 0)),
            pl.BlockSpec(memory_space=pl.ANY),
        ],
        out_specs=pl.BlockSpec((tokens, d), lambda i: (i, 0)),
        out_shape=jax.ShapeDtypeStruct((n, d), F32),
        scratch_shapes=[pltpu.VMEM((2, TOP_K, tokens, d), F32), pltpu.SemaphoreType.DMA((2,))],
        compiler_params=_cparams(("arbitrary",)),
        name="moe_combine",
    )(idx, idx, x1, gate, ybuf)


def _route(experts, n_exp, n_blocks):
    flat_e = experts.reshape(-1)
    onehot = (flat_e[:, None] == jnp.arange(n_exp, dtype=jnp.int32)[None, :]).astype(jnp.int32)
    csum = jnp.cumsum(onehot, axis=0)
    counts = csum[-1]
    padded = (counts + MOE_BLOCK - 1) // MOE_BLOCK * MOE_BLOCK
    pend = jnp.cumsum(padded)
    pstart = pend - padded
    dest = jnp.sum(onehot * (csum - 1 + pstart[None, :]), axis=1)
    block_start = jnp.arange(n_blocks, dtype=jnp.int32) * MOE_BLOCK
    block_e = jnp.minimum(jnp.sum(block_start[:, None] >= pend[None, :], axis=1), n_exp - 1)
    n_valid = jnp.clip(counts[block_e] - (block_start - pstart[block_e]), 0, MOE_BLOCK)
    n_used = jnp.maximum(pend[-1] // MOE_BLOCK, 1)
    block_e = block_e[jnp.minimum(jnp.arange(n_blocks), n_used - 1)]
    i32 = lambda a: a.astype(jnp.int32)
    return i32(dest), i32(block_e), i32(n_valid), i32(n_used).reshape(1)


def _mixers(x, cache, s0, shift_prev, wts):
    b, t, d = x.shape
    width = wts["width"]
    u, q, k, v, k16, v16 = _in_proj(x.reshape(b * t, d), wts["norm1_g"], wts["w_all"], wts["qg"],
                                    wts["kg"], wts["n_u"], wts["n_h"])
    uw = u.shape[1]
    u = u.reshape(b, t, uw)
    q, k, v, k16, v16 = (a.reshape(b, t, width) for a in (q, k, v, k16, v16))
    if cache is None:
        o, _ = _sb_attend(q, k16, v16, wts["sb_o_g"])
    else:
        def past(rows):
            return tuple(c[:, c.shape[1] - rows:].astype(BF16).reshape(b, rows, width) for c in cache)

        n_rows = cache[0].shape[1]
        if n_rows <= SB_PAST_WINDOW:
            o, _ = _sb_attend(q, k16, v16, wts["sb_o_g"], *past(n_rows))
        else:
            o, live = _sb_attend(q, k16, v16, wts["sb_o_g"], *past(SB_PAST_WINDOW))
            o = lax.cond(jnp.any(live > 0),
                         lambda: _sb_attend(q, k16, v16, wts["sb_o_g"], *past(n_rows))[0],
                         lambda: o)

    shift_p = jnp.pad(shift_prev, ((0, 0), (0, uw - shift_prev.shape[1]))).reshape(b, 1, uw)
    r, lw, kf, vv, nkk, bb, g = _rwkv_prep(u, shift_p, wts["mu"], wts["w0"], wts["a0"], wts["k_k"],
                                           wts["k_a"], wts["w2p"], wts["a2p"], wts["g2p"], width)
    y_rwkv, h_fin = _rwkv_chunk(r, lw, kf, vv, nkk, bb, g, _state_to_blockdiag(s0),
                                wts["r_k"], wts["ln_g"], wts["ln_b"])
    x1, h2, experts, gates = _out_proj(x.reshape(b * t, d), y_rwkv.reshape(b * t, width),
                                       o.reshape(b * t, width), wts["w_out"], wts["norm2_g"],
                                       wts["rw"], wts["rb"])
    new_k = k.reshape(b, t, width // HEAD_DIM, HEAD_DIM)
    new_v = v.reshape(b, t, width // HEAD_DIM, HEAD_DIM)
    new_shift = u[:, -1, :wts["rwkv_in"]]
    return x1, h2, experts[:, :TOP_K], gates, new_k, new_v, _blockdiag_to_state(h_fin), new_shift


def kernel(x_prompt, x_sample, cache_sb_k, cache_sb_v, state_rwkv, state_shift, norm1_g, w_in, rwkv_mu, rwkv_w0, rwkv_w2, rwkv_a0, rwkv_a2, rwkv_g2, rwkv_k_k, rwkv_k_a, rwkv_r_k, rwkv_ln_g, rwkv_ln_b, sb_q_g, sb_k_g, sb_o_g, w_out, norm2_g, router_w, router_b, moe_w1, moe_b1, moe_w2, moe_b2):
    depth = w_in.shape[0]
    assert depth == 1
    layer = 0
    d = x_prompt.shape[-1]
    width = rwkv_w0.shape[-1]
    rwkv_in = rwkv_mu.shape[-1]
    n_exp = router_w.shape[-1]
    decay_lora, aaa_lora, gate_lora = rwkv_w2.shape[1], rwkv_a2.shape[1], rwkv_g2.shape[1]
    assert decay_lora + aaa_lora == LANES and width % PROJ_TILE == 0
    n_h = width // PROJ_TILE
    n_u = -(-rwkv_in // PROJ_TILE)
    uw = n_u * PROJ_TILE
    tail_w = uw - 3 * width

    w_l = w_in[layer]
    w_all = jnp.concatenate(
        [w_l[:, :rwkv_in], jnp.zeros((d, uw - rwkv_in), F32), w_l[:, rwkv_in:]], axis=1).astype(BF16)
    reps = PROJ_TILE // HEAD_DIM
    row = lambda a: a.reshape(1, -1).astype(F32)
    wts = dict(
        width=width, rwkv_in=rwkv_in, n_u=n_u, n_h=n_h,
        norm1_g=row(norm1_g[layer]), w_all=w_all,
        qg=row(jnp.tile(sb_q_g[layer], reps)), kg=row(jnp.tile(sb_k_g[layer], reps)),
        sb_o_g=row(sb_o_g[layer]),
        mu=jnp.pad(row(rwkv_mu[layer]), ((0, 0), (0, uw - rwkv_in))),
        w0=row(rwkv_w0[layer]), a0=row(rwkv_a0[layer]),
        k_k=row(rwkv_k_k[layer]), k_a=row(rwkv_k_a[layer]),
        w2p=jnp.pad(rwkv_w2[layer], ((0, aaa_lora), (0, 0))).astype(BF16),
        a2p=jnp.pad(rwkv_a2[layer], ((decay_lora, 0), (0, 0))).astype(BF16),
        g2p=jnp.pad(rwkv_g2[layer], ((0, tail_w - LANES - gate_lora), (0, 0))).astype(BF16),
        r_k=row(rwkv_r_k[layer]), ln_g=row(rwkv_ln_g[layer]), ln_b=row(rwkv_ln_b[layer]),
        w_out=w_out[layer].astype(BF16), norm2_g=row(norm2_g[layer]),
        rw=jnp.concatenate(_split_bf16(jnp.pad(router_w[layer], ((0, 0), (0, LANES - n_exp))), 2), axis=1),
        rb=jnp.pad(row(router_b[layer]), ((0, 0), (0, LANES - n_exp)), constant_values=-jnp.inf),
    )

    bp, tp, _ = x_prompt.shape
    bs, ts, _ = x_sample.shape
    heads = width // HEAD_DIM
    s0_prompt = jnp.zeros((bp, heads, HEAD_DIM, HEAD_DIM), F32)
    shift0_prompt = jnp.zeros((bp, rwkv_in), F32)
    grp_p = _mixers(x_prompt, None, s0_prompt, shift0_prompt, wts)
    grp_s = _mixers(x_sample, (cache_sb_k[layer], cache_sb_v[layer]), state_rwkv[layer],
                    state_shift[layer], wts)

    n_p, n_s = bp * tp, bs * ts
    n_tok = n_p + n_s
    h2 = jnp.concatenate([grp_p[1], grp_s[1]], axis=0)
    experts = jnp.concatenate([grp_p[2], grp_s[2]], axis=0)
    n_blocks = -(-(n_tok * TOP_K) // MOE_BLOCK) + n_exp
    dest, block_e, n_valid, n_used = _route(experts, n_exp, n_blocks)
    xbuf, w1b, w2b = _dispatch(h2, dest, n_valid, moe_w1[layer], moe_w2[layer])
    ybuf = _moe_ffn(xbuf, block_e, n_valid, n_used, w1b, moe_b1[layer], w2b, moe_b2[layer])
    y_p = _combine(grp_p[0], grp_p[3], dest[:n_p * TOP_K], ybuf).reshape(bp, tp, d)
    y_s = _combine(grp_s[0], grp_s[3], dest[n_p * TOP_K:], ybuf).reshape(bs, ts, d)

    lead = lambda a: a[None]
    return (y_p, y_s,
            lead(grp_p[4]), lead(grp_p[5]), lead(grp_p[6]), lead(grp_p[7]),
            lead(grp_s[4]), lead(grp_s[5]), lead(grp_s[6]), lead(grp_s[7]))
```

```python
import functools

import jax
import jax.numpy as jnp
from jax import lax
from jax.experimental import pallas as pl
from jax.experimental.pallas import tpu as pltpu

F32 = jnp.float32
BF16 = jnp.bfloat16

LANES = 128
SUBLANES = 8
VMEM_LIMIT_BYTES = 56 * 1024 * 1024

HEAD_DIM = 64
HEADS_PER_VREG = LANES // HEAD_DIM
TOP_K = 4
RMS_EPS = 1e-5
GN_EPS = 64e-5
SWIGLU_ALPHA = 1.702
SWIGLU_LIMIT = 7.0
KK_NORM_FLOOR = 1e-12

PROJ_TILE = 512
ROW_TILE = 1024
CHUNK = 64
SB_BLOCK = 128
SB_DEAD_LOGIT = -104.0
SB_PAST_WINDOW = 512
MOE_BLOCK = 512
MOE_FF_TILE = 1024
MOE_ROW_SPLIT = 2
GATHER_TOKENS = 64


def _cparams(sem):
    return pltpu.CompilerParams(dimension_semantics=sem, vmem_limit_bytes=VMEM_LIMIT_BYTES)


def _sigmoid(x):
    return 1.0 / (1.0 + jnp.exp(-x))


def _softplus(x):
    return jnp.maximum(x, 0.0) + jnp.log(1.0 + jnp.exp(-jnp.abs(x)))


def _split_bf16(x, parts):
    out = []
    rem = x
    for _ in range(parts):
        p = rem.astype(BF16)
        out.append(p)
        rem = rem - p.astype(F32)
    return out


def _dot(a, b):
    return jnp.dot(a, b, preferred_element_type=F32)


def _dot_nt(a, b):
    return lax.dot_general(a, b, (((1,), (1,)), ((), ())), preferred_element_type=F32)


def _dot_tn(a, b):
    return lax.dot_general(a, b, (((0,), (0,)), ((), ())), preferred_element_type=F32)


def _head_ones(n=LANES):
    r = lax.broadcasted_iota(jnp.int32, (n, n), 0) // HEAD_DIM
    c = lax.broadcasted_iota(jnp.int32, (n, n), 1) // HEAD_DIM
    return (r == c).astype(BF16)


def _head_sum(x, ones, parts=1):
    acc = None
    for p in _split_bf16(x, parts):
        t = _dot(p, ones)
        acc = t if acc is None else acc + t
    return acc


def _in_proj_kernel(x_ref, g_ref, w_ref, qg_ref, kg_ref, u_ref, q_ref, k_ref, v_ref, k16_ref, v16_ref,
                    h_scr, *, n_u, n_h):
    j = pl.program_id(1)

    @pl.when(j == 0)
    def _():
        x = x_ref[...]
        ms = jnp.mean(x * x, axis=-1, keepdims=True)
        h_scr[...] = (x * lax.rsqrt(ms + RMS_EPS) * g_ref[...]).astype(BF16)

    acc = _dot(h_scr[...], w_ref[...])

    def head_norm(y, gain):
        ones = _head_ones()
        cols = []
        for c in range(y.shape[1] // LANES):
            yc = y[:, c * LANES:(c + 1) * LANES]
            ms = _head_sum(yc * yc, ones) * (1.0 / HEAD_DIM)
            cols.append(yc * lax.rsqrt(ms + RMS_EPS))
        return jnp.concatenate(cols, axis=1) * gain

    @pl.when(j < n_u)
    def _():
        u_ref[...] = acc

    @pl.when((j >= n_u) & (j < n_u + n_h))
    def _():
        q_ref[...] = head_norm(acc, qg_ref[...])

    @pl.when((j >= n_u + n_h) & (j < n_u + 2 * n_h))
    def _():
        kn = head_norm(acc, kg_ref[...])
        k_ref[...] = kn
        k16_ref[...] = kn.astype(BF16)

    @pl.when(j >= n_u + 2 * n_h)
    def _():
        v_ref[...] = acc
        v16_ref[...] = acc.astype(BF16)


def _in_proj(x2, norm_g, w_all, qg, kg, n_u, n_h):
    n, d = x2.shape
    tm = min(ROW_TILE, n)
    tn = PROJ_TILE
    n_tiles = n_u + 3 * n_h
    assert w_all.shape == (d, n_tiles * tn) and n % tm == 0
    hw = n_h * tn

    def clip_map(lo, cnt):
        return lambda i, j: (i, jnp.clip(j - lo, 0, cnt - 1))

    return pl.pallas_call(
        functools.partial(_in_proj_kernel, n_u=n_u, n_h=n_h),
        grid=(n // tm, n_tiles),
        in_specs=[
            pl.BlockSpec((tm, d), lambda i, j: (i, 0)),
            pl.BlockSpec((1, d), lambda i, j: (0, 0)),
            pl.BlockSpec((d, tn), lambda i, j: (0, j)),
            pl.BlockSpec((1, tn), lambda i, j: (0, 0)),
            pl.BlockSpec((1, tn), lambda i, j: (0, 0)),
        ],
        out_specs=[
            pl.BlockSpec((tm, tn), clip_map(0, n_u)),
            pl.BlockSpec((tm, tn), clip_map(n_u, n_h)),
            pl.BlockSpec((tm, tn), clip_map(n_u + n_h, n_h)),
            pl.BlockSpec((tm, tn), clip_map(n_u + 2 * n_h, n_h)),
            pl.BlockSpec((tm, tn), clip_map(n_u + n_h, n_h)),
            pl.BlockSpec((tm, tn), clip_map(n_u + 2 * n_h, n_h)),
        ],
        out_shape=[
            jax.ShapeDtypeStruct((n, n_u * tn), F32),
            jax.ShapeDtypeStruct((n, hw), F32),
            jax.ShapeDtypeStruct((n, hw), F32),
            jax.ShapeDtypeStruct((n, hw), F32),
            jax.ShapeDtypeStruct((n, hw), BF16),
            jax.ShapeDtypeStruct((n, hw), BF16),
        ],
        scratch_shapes=[pltpu.VMEM((tm, d), BF16)],
        compiler_params=_cparams(("parallel", "arbitrary")),
        name="in_proj",
    )(x2, norm_g, w_all, qg, kg)


def _sb_kernel(*refs, tq, n_past, n_grp):
    if n_past:
        q_ref, k_ref, v_ref, pk_ref, pv_ref, og_ref, o_ref, live_ref, acc_ref = refs
    else:
        q_ref, k_ref, v_ref, og_ref, o_ref, live_ref, acc_ref = refs
        pk_ref = pv_ref = None
    qi = pl.program_id(2)
    n_heads = HEADS_PER_VREG * n_grp
    lane = lax.broadcasted_iota(jnp.int32, (1, LANES), 1)
    in_a = lane < HEAD_DIM
    q_heads = []
    for g in range(n_grp):
        q = q_ref[0, :, g * LANES:(g + 1) * LANES] * (HEAD_DIM ** -0.5)
        q_heads.append(jnp.where(in_a, q, 0.0).astype(BF16))
        q_heads.append(jnp.where(in_a, 0.0, q).astype(BF16))
    acc_ref[...] = jnp.zeros_like(acc_ref)

    def visit(k_src, v_src, r0, tk, carries, mask):
        rows = pl.ds(r0, tk)
        s_idx = lax.broadcasted_iota(jnp.int32, (tk, tk), 0)
        j_idx = lax.broadcasted_iota(jnp.int32, (tk, tk), 1)
        suffix = (s_idx >= j_idx).astype(BF16)
        kbs = [k_src[0, rows, g * LANES:(g + 1) * LANES] for g in range(n_grp)]
        zs = [_dot_nt(q_heads[h], kbs[h // HEADS_PER_VREG]) for h in range(n_heads)]
        splits = []
        for z in zs:
            log_1mb = -_softplus(z)
            if mask is not None:
                log_1mb = jnp.where(mask, log_1mb, 0.0)
            splits.append(_split_bf16(log_1mb, 2))
        incls = [_dot(hi, suffix) + _dot(lo, suffix) for hi, lo in splits]
        ps = []
        for z, incl, carry in zip(zs, incls, carries):
            p = jnp.exp(z + incl + carry)
            if mask is not None:
                p = jnp.where(mask, p, 0.0)
            ps.append(p.astype(BF16))
        vbs = [v_src[0, rows, g * LANES:(g + 1) * LANES] for g in range(n_grp)]
        for h in range(n_heads):
            acc_ref[h] += _dot(ps[h], vbs[h // HEADS_PER_VREG])
        return tuple(c + incl[:, 0:1] for c, incl in zip(carries, incls))

    q0 = pl.multiple_of(qi * tq, tq)
    t_idx = lax.broadcasted_iota(jnp.int32, (tq, tq), 0)
    j_idx = lax.broadcasted_iota(jnp.int32, (tq, tq), 1)
    zero = jnp.zeros((tq, 1), F32)
    carries = visit(k_ref, v_ref, q0, tq, (zero,) * n_heads, j_idx < t_idx)

    def alive(carries):
        top = carries[0]
        for c in carries[1:]:
            top = jnp.maximum(top, c)
        return (jnp.max(top) > SB_DEAD_LOGIT).astype(jnp.int32)

    def sweep(kr, vr, n_blocks, carries):
        def cond(st):
            return (st[0] >= 0) & (st[1] > 0)

        def body(st):
            b = st[0]
            r0 = pl.multiple_of(b * SB_BLOCK, SB_BLOCK)
            c = visit(kr, vr, r0, SB_BLOCK, st[2:], None)
            return (b - 1, alive(c)) + c

        st = lax.while_loop(cond, body, (n_blocks - 1, alive(carries)) + carries)
        return st[2:]

    if tq == SB_BLOCK:
        carries = sweep(k_ref, v_ref, qi, carries)
    if n_past:
        carries = sweep(pk_ref, pv_ref, jnp.int32(n_past), carries)
    live_ref[...] = jnp.broadcast_to(alive(carries), live_ref.shape)

    ones = _head_ones()
    for g in range(n_grp):
        o = jnp.where(in_a, acc_ref[HEADS_PER_VREG * g], acc_ref[HEADS_PER_VREG * g + 1])
        ms = _head_sum(o * o, ones) * (1.0 / HEAD_DIM)
        cs = slice(g * LANES, (g + 1) * LANES)
        o_ref[0, :, cs] = o * lax.rsqrt(ms + RMS_EPS) * og_ref[:, cs]


def _sb_attend(q, k, v, og, past_k=None, past_v=None):
    b, t, w = q.shape
    n_grp = 4
    gw = n_grp * LANES
    assert w % gw == 0
    tq = min(SB_BLOCK, t)
    assert t % tq == 0 and (t == tq or tq == SB_BLOCK)
    n_past = 0
    in_specs = [
        pl.BlockSpec((1, tq, gw), lambda bi, p, qi: (bi, qi, p)),
        pl.BlockSpec((1, t, gw), lambda bi, p, qi: (bi, 0, p)),
        pl.BlockSpec((1, t, gw), lambda bi, p, qi: (bi, 0, p)),
    ]
    args = [q, k, v]
    if past_k is not None:
        tp = past_k.shape[1]
        assert tp % SB_BLOCK == 0 and t == tq
        n_past = tp // SB_BLOCK
        in_specs += [pl.BlockSpec((1, tp, gw), lambda bi, p, qi: (bi, 0, p))] * 2
        args += [past_k, past_v]
    in_specs.append(pl.BlockSpec((1, gw), lambda bi, p, qi: (0, p)))
    args.append(og)
    return pl.pallas_call(
        functools.partial(_sb_kernel, tq=tq, n_past=n_past, n_grp=n_grp),
        grid=(b, w // gw, t // tq),
        in_specs=in_specs,
        out_specs=[pl.BlockSpec((1, tq, gw), lambda bi, p, qi: (bi, qi, p)),
                   pl.BlockSpec((1, 1, 1, SUBLANES, LANES), lambda bi, p, qi: (bi, p, qi, 0, 0))],
        out_shape=[jax.ShapeDtypeStruct((b, t, w), F32),
                   jax.ShapeDtypeStruct((b, w // gw, t // tq, SUBLANES, LANES), jnp.int32)],
        scratch_shapes=[pltpu.VMEM((HEADS_PER_VREG * n_grp, tq, LANES), F32)],
        compiler_params=_cparams(("parallel", "parallel", "arbitrary")),
        name="sb_attend",
    )(*args)


def _rwkv_prep_kernel(u_ref, sh_ref, mu_ref, w0_ref, a0_ref, kk_ref, ka_ref, w2_ref, a2_ref, g2_ref,
                      r_o, lw_o, kf_o, v_o, nkk_o, b_o, g_o, prev_scr, *, width):
    ti = pl.program_id(1)
    tt = u_ref.shape[1]

    @pl.when(ti == 0)
    def _():
        prev_scr[...] = sh_ref[0]

    row = lax.broadcasted_iota(jnp.int32, (tt, 1), 0)

    def mixed(lo, hi):
        u_raw = u_ref[0, :, lo:hi]
        shifted = jnp.where(row == 0, prev_scr[:, lo:hi], pltpu.roll(u_raw, 1, axis=0))
        return u_raw + (shifted - u_raw) * mu_ref[:, lo:hi]

    r = mixed(0, width)
    k = mixed(width, 2 * width)
    v = mixed(2 * width, 3 * width)
    tail = mixed(3 * width, u_ref.shape[2])
    prev_scr[...] = u_ref[0, tt - 1:tt, :]

    lora_in = tail[:, :LANES]
    dw = _dot(jnp.tanh(lora_in).astype(BF16), w2_ref[...])
    da = _dot(lora_in.astype(BF16), a2_ref[...])
    g = _dot(_sigmoid(tail[:, LANES:]).astype(BF16), g2_ref[...])
    w_log = -_softplus(-(w0_ref[...] + dw)) - 0.5
    a = _sigmoid(a0_ref[...] + da)

    kk = k * kk_ref[...]
    ones = _head_ones()
    cols = []
    for c in range(width // LANES):
        kc = kk[:, c * LANES:(c + 1) * LANES]
        norm = jnp.sqrt(_head_sum(kc * kc, ones))
        cols.append(kc / jnp.maximum(norm, KK_NORM_FLOOR))
    kk = jnp.concatenate(cols, axis=1)

    r_o[0] = r
    lw_o[0] = -jnp.exp(w_log)
    kf_o[0] = k * (1.0 + (a - 1.0) * ka_ref[...])
    v_o[0] = v
    nkk_o[0] = -kk
    b_o[0] = kk * a
    g_o[0] = g


def _rwkv_prep(u, shift_prev, mu, w0, a0, k_k, k_a, w2p, a2p, g2p, width):
    b, t, uw = u.shape
    tt = min(256, t)
    assert t % tt == 0
    row_spec = pl.BlockSpec((1, width), lambda bi, ti: (0, 0))
    out_spec = pl.BlockSpec((1, tt, width), lambda bi, ti: (bi, ti, 0))
    return pl.pallas_call(
        functools.partial(_rwkv_prep_kernel, width=width),
        grid=(b, t // tt),
        in_specs=[
            pl.BlockSpec((1, tt, uw), lambda bi, ti: (bi, ti, 0)),
            pl.BlockSpec((1, 1, uw), lambda bi, ti: (bi, 0, 0)),
            pl.BlockSpec((1, uw), lambda bi, ti: (0, 0)),
            row_spec, row_spec, row_spec, row_spec,
            pl.BlockSpec(w2p.shape, lambda bi, ti: (0, 0)),
            pl.BlockSpec(a2p.shape, lambda bi, ti: (0, 0)),
            pl.BlockSpec(g2p.shape, lambda bi, ti: (0, 0)),
        ],
        out_specs=[out_spec] * 7,
        out_shape=[jax.ShapeDtypeStruct((b, t, width), F32)] * 7,
        scratch_shapes=[pltpu.VMEM((1, uw), F32)],
        compiler_params=_cparams(("parallel", "arbitrary")),
        name="rwkv_prep",
    )(u, shift_prev, mu, w0, a0, k_k, k_a, w2p, a2p, g2p)


def _rwkv_chunk_kernel(r_ref, lw_ref, k_ref, v_ref, nkk_ref, b_ref, g_ref, h0_ref,
                       rk_ref, lng_ref, lnb_ref, y_ref, hout_ref, h_scr, *, n_bat, n_pairs, n_chunks):
    ti = pl.program_id(1)
    c_len = CHUNK

    @pl.when(ti == 0)
    def _():
        h_scr[...] = h0_ref[...]

    lane = lax.broadcasted_iota(jnp.int32, (1, LANES), 1)
    head_lanes = (lane < HEAD_DIM, lane >= HEAD_DIM)
    t_idx = lax.broadcasted_iota(jnp.int32, (c_len, c_len), 0)
    i_idx = lax.broadcasted_iota(jnp.int32, (c_len, c_len), 1)
    strict = i_idx < t_idx
    causal = i_idx <= t_idx
    prefix = causal.astype(BF16)
    eye_c = (i_idx == t_idx).astype(F32)
    r128 = lax.broadcasted_iota(jnp.int32, (LANES, LANES), 0)
    c128 = lax.broadcasted_iota(jnp.int32, (LANES, LANES), 1)
    same_head = (r128 // HEAD_DIM) == (c128 // HEAD_DIM)
    eye128 = r128 == c128
    ones = _head_ones()
    n_sub = HEADS_PER_VREG

    def chunk_group(bi, r0):
        rows = pl.ds(r0, c_len)
        pairs = range(n_pairs)
        cols = [slice(pp * LANES, (pp + 1) * LANES) for pp in pairs]
        heads = [(pp, s) for pp in pairs for s in range(n_sub)]

        cums = []
        for pp in pairs:
            parts = _split_bf16(lw_ref[bi, rows, cols[pp]], 3)
            cums.append(_dot(prefix, parts[0]) + _dot(prefix, parts[1]) + _dot(prefix, parts[2]))

        a_ts, r_ts, b16s, k16s, v16s, bks, p_ends = [], [], [], [], [], [], []
        for pp in pairs:
            cs = cols[pp]
            cum = cums[pp]
            lw = lw_ref[bi, rows, cs]
            kf = k_ref[bi, rows, cs]
            bb = b_ref[bi, rows, cs]
            p_inv = jnp.exp(-cum)
            p_end = cum[c_len - 1:c_len, :]
            to_end = jnp.exp(p_end - cum)
            a_ts.append(nkk_ref[bi, rows, cs] * jnp.exp(cum - lw))
            r_ts.append(r_ref[bi, rows, cs] * jnp.exp(cum))
            b16s.append((bb * p_inv).astype(BF16))
            k16s.append((kf * p_inv).astype(BF16))
            v16s.append(v_ref[bi, rows, cs].astype(BF16))
            bks.append(jnp.concatenate([bb * to_end, kf * to_end], axis=0).astype(BF16))
            p_ends.append(p_end)

        m_bs, m_ks = [], []
        for pp, s in heads:
            sel = head_lanes[s]
            lhs = jnp.concatenate([jnp.where(sel, a_ts[pp], 0.0), jnp.where(sel, r_ts[pp], 0.0)],
                                  axis=0).astype(BF16)
            m_bs.append(_dot_nt(lhs, b16s[pp]))
            m_ks.append(_dot_nt(lhs, k16s[pp]))

        a_rbs = [jnp.where(causal, m[c_len:], 0.0).astype(BF16) for m in m_bs]
        a_rks = [jnp.where(causal, m[c_len:], 0.0).astype(BF16) for m in m_ks]
        avs = [_dot(jnp.where(strict, m[:c_len], 0.0).astype(BF16), v16s[pp])
               for m, (pp, _) in zip(m_ks, heads)]
        n_pows = [jnp.where(strict, m[:c_len], 0.0) for m in m_bs]
        t_invs = [eye_c + n for n in n_pows]
        span = 2
        while span < c_len:
            n16s = [n.astype(BF16) for n in n_pows]
            n_pows = [_dot(n, n) for n in n16s]
            t_invs = [_dot(t.astype(BF16), (eye_c + n).astype(BF16)) for t, n in zip(t_invs, n_pows)]
            span *= 2

        wus = [_dot(t.astype(BF16), jnp.concatenate([a_ts[pp], av], axis=1).astype(BF16))
               for t, av, (pp, _) in zip(t_invs, avs, heads)]
        zqs = [_dot(a, wu.astype(BF16)) for a, wu in zip(a_rbs, wus)]
        yks = [_dot(a, v16s[pp]) for a, (pp, _) in zip(a_rks, heads)]

        ys, gjs = [], []
        for pp in pairs:
            h0, h1 = n_sub * pp, n_sub * pp + 1
            sel_a = head_lanes[0]
            w2 = jnp.where(sel_a, wus[h0][:, :LANES], wus[h1][:, :LANES])
            u0 = jnp.where(sel_a, wus[h0][:, LANES:], wus[h1][:, LANES:])
            q2 = r_ts[pp] + jnp.where(sel_a, zqs[h0][:, :LANES], zqs[h1][:, :LANES])
            y0 = jnp.where(sel_a, zqs[h0][:, LANES:] + yks[h0], zqs[h1][:, LANES:] + yks[h1])
            h16 = h_scr[bi, pp].astype(BF16)
            ys.append(_dot(q2.astype(BF16), h16) + y0)
            top = jnp.concatenate([w2, u0], axis=1).astype(BF16)
            bot = jnp.concatenate([jnp.zeros((c_len, LANES), BF16), v16s[pp]], axis=1)
            gjs.append(_dot_tn(bks[pp], jnp.concatenate([top, bot], axis=0)))

        for pp in pairs:
            gj = gjs[pp]
            g_mat = jnp.where(eye128, jnp.exp(p_ends[pp]), 0.0) + jnp.where(same_head, gj[:, :LANES], 0.0)
            h16 = h_scr[bi, pp].astype(BF16)
            h_scr[bi, pp] = _dot(g_mat.astype(BF16), h16) + jnp.where(same_head, gj[:, LANES:], 0.0)

        means = [_head_sum(y, ones, parts=2) * (1.0 / HEAD_DIM) for y in ys]
        ycs = [y - m for y, m in zip(ys, means)]
        variances = [_head_sum(yc * yc, ones) * (1.0 / HEAD_DIM) for yc in ycs]
        bonuses = [_head_sum(r_ref[bi, rows, cols[pp]] * k_ref[bi, rows, cols[pp]] * rk_ref[:, cols[pp]], ones)
                   for pp in pairs]
        for pp in pairs:
            cs = cols[pp]
            yn = ycs[pp] * lax.rsqrt(variances[pp] + GN_EPS) * lng_ref[:, cs] + lnb_ref[:, cs]
            y_ref[bi, rows, cs] = (yn + bonuses[pp] * v_ref[bi, rows, cs]) * g_ref[bi, rows, cs]

    def chunk_step(ci, carry):
        r0 = pl.multiple_of(ci * c_len, c_len)
        for bi in range(n_bat):
            chunk_group(bi, r0)
        return carry

    lax.fori_loop(0, n_chunks, chunk_step, 0)

    @pl.when(ti == pl.num_programs(1) - 1)
    def _():
        hout_ref[...] = h_scr[...]


def _rwkv_chunk(r, lw, kf, v, nkk, bb, g, h0, r_k, ln_g, ln_b):
    b, t, w = r.shape
    pairs = w // LANES
    tt = min(2 * CHUNK, t)
    n_bat = 2 if b % 2 == 0 else 1
    assert t % tt == 0 and tt % CHUNK == 0 and b % n_bat == 0
    seq = pl.BlockSpec((n_bat, tt, w), lambda bi, ti: (bi, ti, 0))
    par = pl.BlockSpec((1, w), lambda bi, ti: (0, 0))
    st = pl.BlockSpec((n_bat, pairs, LANES, LANES), lambda bi, ti: (bi, 0, 0, 0))
    return pl.pallas_call(
        functools.partial(_rwkv_chunk_kernel, n_bat=n_bat, n_pairs=pairs, n_chunks=tt // CHUNK),
        grid=(b // n_bat, t // tt),
        in_specs=[seq] * 7 + [st, par, par, par],
        out_specs=[seq, st],
        out_shape=[jax.ShapeDtypeStruct((b, t, w), F32),
                   jax.ShapeDtypeStruct((b, pairs, LANES, LANES), F32)],
        scratch_shapes=[pltpu.VMEM((n_bat, pairs, LANES, LANES), F32)],
        compiler_params=_cparams(("parallel", "arbitrary")),
        name="rwkv_chunk",
    )(r, lw, kf, v, nkk, bb, g, h0, r_k, ln_g, ln_b)


def _state_to_blockdiag(s):
    b, h, dv, dk = s.shape
    st = jnp.swapaxes(s, -1, -2).reshape(b, h // 2, 2, dk, dv)
    z = jnp.zeros_like(st[:, :, 0])
    top = jnp.concatenate([st[:, :, 0], z], axis=-1)
    bot = jnp.concatenate([z, st[:, :, 1]], axis=-1)
    return jnp.concatenate([top, bot], axis=-2)


def _blockdiag_to_state(hm):
    b, p, _, _ = hm.shape
    a = hm[:, :, :HEAD_DIM, :HEAD_DIM]
    c = hm[:, :, HEAD_DIM:, HEAD_DIM:]
    st = jnp.stack([a, c], axis=2).reshape(b, 2 * p, HEAD_DIM, HEAD_DIM)
    return jnp.swapaxes(st, -1, -2)


def _out_proj_kernel(x_ref, yr_ref, o_ref, w_ref, g_ref, rw_ref, rb_ref,
                     x1_ref, h2_ref, e_ref, gate_ref):
    half = yr_ref.shape[1]
    x1 = (x_ref[...] + _dot(yr_ref[...].astype(BF16), w_ref[:half, :])
          + _dot(o_ref[...].astype(BF16), w_ref[half:, :]))
    x1_ref[...] = x1
    ms = jnp.mean(x1 * x1, axis=-1, keepdims=True)
    h2 = x1 * lax.rsqrt(ms + RMS_EPS) * g_ref[...]
    h2_ref[...] = h2
    h_hi, h_lo = _split_bf16(h2, 2)
    both = _dot(h_hi, rw_ref[...])
    logits = both[:, :LANES] + both[:, LANES:] + _dot(h_lo, rw_ref[:, :LANES]) + rb_ref[...]
    lane = lax.broadcasted_iota(jnp.int32, logits.shape, 1)
    lane_f = lane.astype(F32)
    experts = jnp.zeros(logits.shape, F32)
    tops = jnp.zeros(logits.shape, F32)
    work = logits
    top0 = None
    for kth in range(TOP_K):
        m = jnp.max(work, axis=-1, keepdims=True)
        idx = jnp.min(jnp.where(work == m, lane_f, float(LANES)), axis=-1, keepdims=True)
        if top0 is None:
            top0 = m
        experts = jnp.where(lane == kth, idx, experts)
        tops = jnp.where(lane == kth, jnp.exp(m - top0), tops)
        work = jnp.where(lane_f == idx, -jnp.inf, work)
    e_ref[...] = experts.astype(jnp.int32)
    gate_ref[...] = tops / jnp.sum(tops, axis=-1, keepdims=True)


def _out_proj(x2, yr, o, w_out, norm_g, rw, rb):
    n, d = x2.shape
    tm = min(256, n)
    assert n % tm == 0
    half = yr.shape[1]
    row = lambda i: (i, 0)
    fixed = lambda i: (0, 0)
    return pl.pallas_call(
        _out_proj_kernel,
        grid=(n // tm,),
        in_specs=[
            pl.BlockSpec((tm, d), row),
            pl.BlockSpec((tm, half), row),
            pl.BlockSpec((tm, half), row),
            pl.BlockSpec(w_out.shape, fixed),
            pl.BlockSpec((1, d), fixed),
            pl.BlockSpec(rw.shape, fixed),
            pl.BlockSpec((1, LANES), fixed),
        ],
        out_specs=[
            pl.BlockSpec((tm, d), row),
            pl.BlockSpec((tm, d), row),
            pl.BlockSpec((tm, LANES), row),
            pl.BlockSpec((tm, LANES), row),
        ],
        out_shape=[
            jax.ShapeDtypeStruct((n, d), F32),
            jax.ShapeDtypeStruct((n, d), F32),
            jax.ShapeDtypeStruct((n, LANES), jnp.int32),
            jax.ShapeDtypeStruct((n, LANES), F32),
        ],
        compiler_params=_cparams(("parallel",)),
        name="out_proj",
    )(x2, yr, o, w_out, norm_g, rw, rb)


def _dispatch_kernel(nv_ref, dest_ref, h2_ref, w1_ref, w2_ref, xbuf_hbm, w1b_ref, w2b_ref, zero_scr, sem,
                     *, tokens, n_blocks):
    w1b_ref[...] = w1_ref[...].astype(BF16)
    w2b_ref[...] = w2_ref[...].astype(BF16)

    @pl.when(pl.program_id(0) == 0)
    def _():
        zero_scr[...] = jnp.zeros_like(zero_scr)

        def fill(i):
            r0 = pl.multiple_of(i * MOE_BLOCK, MOE_BLOCK)
            return pltpu.make_async_copy(zero_scr, xbuf_hbm.at[pl.ds(r0, MOE_BLOCK)], sem)

        def fill_start(i, c):
            @pl.when(nv_ref[i] < MOE_BLOCK)
            def _():
                fill(i).start()
            return c

        def fill_wait(i, c):
            @pl.when(nv_ref[i] < MOE_BLOCK)
            def _():
                fill(i).wait()
            return c

        lax.fori_loop(0, n_blocks, fill_start, 0)
        lax.fori_loop(0, n_blocks, fill_wait, 0)

    def copy(t0, t, k):
        return pltpu.make_async_copy(h2_ref.at[pl.ds(t0 + t, 1)],
                                     xbuf_hbm.at[pl.ds(dest_ref[0, 0, (t0 + t) * TOP_K + k], 1)], sem)

    def for_rows(fn):
        def group(g, c):
            t0 = pl.multiple_of(g * SUBLANES, SUBLANES)
            for t in range(SUBLANES):
                for k in range(TOP_K):
                    fn(copy(t0, t, k))
            return c
        lax.fori_loop(0, tokens // SUBLANES, group, 0)

    for_rows(lambda c: c.start())
    for_rows(lambda c: c.wait())


def _dispatch(h2, dest, n_valid, w1, w2):
    n, d = h2.shape
    steps = 1
    while steps < 128 and n % (2 * steps * SUBLANES) == 0:
        steps *= 2
    tokens = n // steps
    rows = tokens * TOP_K
    n_blocks = n_valid.shape[0]
    w1f = w1.reshape(-1, w1.shape[-1])
    w2f = w2.reshape(-1, w2.shape[-1])
    r1, r2 = w1f.shape[0] // steps, w2f.shape[0] // steps
    assert w1f.shape[0] % steps == 0 and w2f.shape[0] % steps == 0 and r1 % 16 == 0 and r2 % 16 == 0
    grid_spec = pltpu.PrefetchScalarGridSpec(
        num_scalar_prefetch=1,
        grid=(steps,),
        in_specs=[
            pl.BlockSpec((1, 1, rows), lambda i, nv: (i, 0, 0), memory_space=pltpu.SMEM),
            pl.BlockSpec((tokens, d), lambda i, nv: (i, 0)),
            pl.BlockSpec((r1, w1f.shape[1]), lambda i, nv: (i, 0)),
            pl.BlockSpec((r2, w2f.shape[1]), lambda i, nv: (i, 0)),
        ],
        out_specs=[
            pl.BlockSpec(memory_space=pl.ANY),
            pl.BlockSpec((r1, w1f.shape[1]), lambda i, nv: (i, 0)),
            pl.BlockSpec((r2, w2f.shape[1]), lambda i, nv: (i, 0)),
        ],
        scratch_shapes=[pltpu.VMEM((MOE_BLOCK, d), F32), pltpu.SemaphoreType.DMA],
    )
    xbuf, w1b, w2b = pl.pallas_call(
        functools.partial(_dispatch_kernel, tokens=tokens, n_blocks=n_blocks),
        grid_spec=grid_spec,
        out_shape=[jax.ShapeDtypeStruct((n_blocks * MOE_BLOCK, d), F32),
                   jax.ShapeDtypeStruct(w1f.shape, BF16),
                   jax.ShapeDtypeStruct(w2f.shape, BF16)],
        compiler_params=_cparams(("arbitrary",)),
        name="moe_dispatch",
    )(n_valid, dest.reshape(steps, 1, rows), h2, w1f, w2f)
    return xbuf, w1b.reshape(w1.shape), w2b.reshape(w2.shape)


def _moe_ffn_kernel(be_ref, nv_ref, nu_ref, x_ref, w1g_ref, w1l_ref, b1g_ref, b1l_ref, w2_ref, b2_ref,
                    y_ref, xs_scr):
    i = pl.program_id(0)
    f = pl.program_id(1)
    n_valid = nv_ref[i]

    @pl.when((n_valid == 0) & (f == 0))
    def _():
        y_ref[...] = jnp.zeros_like(y_ref)

    @pl.when(n_valid > 0)
    def _():
        @pl.when(f == 0)
        def _():
            xs_scr[...] = x_ref[...].astype(BF16)
            y_ref[...] = jnp.broadcast_to(b2_ref[0], y_ref.shape)

        bm = y_ref.shape[0]
        halves = [pl.ds(s * (bm // MOE_ROW_SPLIT), bm // MOE_ROW_SPLIT) for s in range(MOE_ROW_SPLIT)]
        xs = [xs_scr[rows, :] for rows in halves]
        hgs = [_dot(x, w1g_ref[0]) + b1g_ref[0] for x in xs]
        hls = [_dot(x, w1l_ref[0]) + b1l_ref[0] for x in xs]
        acts = []
        for hg, hl in zip(hgs, hls):
            glu = jnp.minimum(hg, SWIGLU_LIMIT)
            lin = jnp.clip(hl, -SWIGLU_LIMIT, SWIGLU_LIMIT)
            acts.append((glu * _sigmoid(SWIGLU_ALPHA * glu) * (lin + 1.0)).astype(BF16))
        parts = [_dot(act, w2_ref[0]) for act in acts]
        for rows, part in zip(halves, parts):
            y_ref[rows, :] += part


def _moe_ffn(xbuf, block_e, n_valid, n_used, w1, b1, w2, b2):
    n_rows, d = xbuf.shape
    n_exp, _, ff2 = w1.shape
    ff = ff2 // 2
    bm, tf = MOE_BLOCK, MOE_FF_TILE
    n_f = ff // tf
    n_blocks = n_rows // bm
    b1r = b1.reshape(n_exp, 1, ff2)
    b2r = b2.reshape(n_exp, 1, d)
    def ff(i, f, nv):
        return jnp.where(nv[i] > 0, f, n_f - 1)

    def xi(i, nv, nu):
        return jnp.minimum(i, nu[0] - 1)

    grid_spec = pltpu.PrefetchScalarGridSpec(
        num_scalar_prefetch=3,
        grid=(n_blocks, n_f),
        in_specs=[
            pl.BlockSpec((bm, d), lambda i, f, be, nv, nu: (xi(i, nv, nu), 0)),
            pl.BlockSpec((1, d, tf), lambda i, f, be, nv, nu: (be[i], 0, ff(i, f, nv))),
            pl.BlockSpec((1, d, tf), lambda i, f, be, nv, nu: (be[i], 0, n_f + ff(i, f, nv))),
            pl.BlockSpec((1, 1, tf), lambda i, f, be, nv, nu: (be[i], 0, ff(i, f, nv))),
            pl.BlockSpec((1, 1, tf), lambda i, f, be, nv, nu: (be[i], 0, n_f + ff(i, f, nv))),
            pl.BlockSpec((1, tf, d), lambda i, f, be, nv, nu: (be[i], ff(i, f, nv), 0)),
            pl.BlockSpec((1, 1, d), lambda i, f, be, nv, nu: (be[i], 0, 0)),
        ],
        out_specs=pl.BlockSpec((bm, d), lambda i, f, be, nv, nu: (i, 0)),
        scratch_shapes=[pltpu.VMEM((bm, d), BF16)],
    )
    return pl.pallas_call(
        _moe_ffn_kernel,
        grid_spec=grid_spec,
        out_shape=jax.ShapeDtypeStruct((n_rows, d), F32),
        compiler_params=_cparams(("parallel", "arbitrary")),
        name="moe_ffn",
    )(block_e, n_valid, n_used, xbuf, w1, w1, b1r, b1r, w2, b2r)


def _combine_kernel(dest_ref, next_ref, x1_ref, gate_ref, ybuf_hbm, out_ref, rows_scr, sems, *, tokens):
    i = pl.program_id(0)
    n = pl.num_programs(0)
    slot = i % 2

    def copy(idx_ref, buf, t0, t, k):
        return pltpu.make_async_copy(ybuf_hbm.at[pl.ds(idx_ref[0, 0, (t0 + t) * TOP_K + k], 1)],
                                     rows_scr.at[buf, k, pl.ds(t0 + t, 1)], sems.at[buf])

    def for_rows(idx_ref, buf, fn):
        def group(g, c):
            t0 = pl.multiple_of(g * SUBLANES, SUBLANES)
            for t in range(SUBLANES):
                for k in range(TOP_K):
                    fn(copy(idx_ref, buf, t0, t, k))
            return c
        lax.fori_loop(0, tokens // SUBLANES, group, 0)

    def start_all(idx_ref, buf):
        for_rows(idx_ref, buf, lambda c: c.start())

    @pl.when(i == 0)
    def _():
        start_all(dest_ref, 0)

    @pl.when(i + 1 < n)
    def _():
        start_all(next_ref, 1 - slot)

    for_rows(dest_ref, slot, lambda c: c.wait())
    acc = x1_ref[...]
    gate = gate_ref[...]
    for kth in range(TOP_K):
        acc = acc + gate[:, kth:kth + 1] * rows_scr[slot, kth]
    out_ref[...] = acc


def _combine(x1, gate, dest, ybuf):
    n, d = x1.shape
    tokens = min(GATHER_TOKENS, n)
    assert n % tokens == 0
    steps = n // tokens
    idx = dest.reshape(steps, 1, tokens * TOP_K)
    idx_block = (1, 1, tokens * TOP_K)
    return pl.pallas_call(
        functools.partial(_combine_kernel, tokens=tokens),
        grid=(steps,),
        in_specs=[
            pl.BlockSpec(idx_block, lambda i: (i, 0, 0), memory_space=pltpu.SMEM),
            pl.BlockSpec(idx_block, lambda i: (jnp.minimum(i + 1, steps - 1), 0, 0), memory_space=pltpu.SMEM),
            pl.BlockSpec((tokens, d), lambda i: (i, 0)),
            pl.BlockSpec((tokens, LANES), lambda i: (i, 0)),
            pl.BlockSpec(memory_space=pl.ANY),
        ],
        out_specs=pl.BlockSpec((tokens, d), lambda i: (i, 0)),
        out_shape=jax.ShapeDtypeStruct((n, d), F32),
        scratch_shapes=[pltpu.VMEM((2, TOP_K, tokens, d), F32), pltpu.SemaphoreType.DMA((2,))],
        compiler_params=_cparams(("arbitrary",)),
        name="moe_combine",
    )(idx, idx, x1, gate, ybuf)


def _route(experts, n_exp, n_blocks):
    flat_e = experts.reshape(-1)
    onehot = (flat_e[:, None] == jnp.arange(n_exp, dtype=jnp.int32)[None, :]).astype(jnp.int32)
    csum = jnp.cumsum(onehot, axis=0)
    counts = csum[-1]
    padded = (counts + MOE_BLOCK - 1) // MOE_BLOCK * MOE_BLOCK
    pend = jnp.cumsum(padded)
    pstart = pend - padded
    dest = jnp.sum(onehot * (csum - 1 + pstart[None, :]), axis=1)
    block_start = jnp.arange(n_blocks, dtype=jnp.int32) * MOE_BLOCK
    block_e = jnp.minimum(jnp.sum(block_start[:, None] >= pend[None, :], axis=1), n_exp - 1)
    n_valid = jnp.clip(counts[block_e] - (block_start - pstart[block_e]), 0, MOE_BLOCK)
    n_used = jnp.maximum(pend[-1] // MOE_BLOCK, 1)
    block_e = block_e[jnp.minimum(jnp.arange(n_blocks), n_used - 1)]
    i32 = lambda a: a.astype(jnp.int32)
    return i32(dest), i32(block_e), i32(n_valid), i32(n_used).reshape(1)


def _mixers(x, cache, s0, shift_prev, wts):
    b, t, d = x.shape
    width = wts["width"]
    u, q, k, v, k16, v16 = _in_proj(x.reshape(b * t, d), wts["norm1_g"], wts["w_all"], wts["qg"],
                                    wts["kg"], wts["n_u"], wts["n_h"])
    uw = u.shape[1]
    u = u.reshape(b, t, uw)
    q, k, v, k16, v16 = (a.reshape(b, t, width) for a in (q, k, v, k16, v16))
    if cache is None:
        o, _ = _sb_attend(q, k16, v16, wts["sb_o_g"])
    else:
        def past(rows):
            return tuple(c[:, c.shape[1] - rows:].astype(BF16).reshape(b, rows, width) for c in cache)

        n_rows = cache[0].shape[1]
        if n_rows <= SB_PAST_WINDOW:
            o, _ = _sb_attend(q, k16, v16, wts["sb_o_g"], *past(n_rows))
        else:
            o, live = _sb_attend(q, k16, v16, wts["sb_o_g"], *past(SB_PAST_WINDOW))
            o = lax.cond(jnp.any(live > 0),
                         lambda: _sb_attend(q, k16, v16, wts["sb_o_g"], *past(n_rows))[0],
                         lambda: o)

    shift_p = jnp.pad(shift_prev, ((0, 0), (0, uw - shift_prev.shape[1]))).reshape(b, 1, uw)
    r, lw, kf, vv, nkk, bb, g = _rwkv_prep(u, shift_p, wts["mu"], wts["w0"], wts["a0"], wts["k_k"],
                                           wts["k_a"], wts["w2p"], wts["a2p"], wts["g2p"], width)
    y_rwkv, h_fin = _rwkv_chunk(r, lw, kf, vv, nkk, bb, g, _state_to_blockdiag(s0),
                                wts["r_k"], wts["ln_g"], wts["ln_b"])
    x1, h2, experts, gates = _out_proj(x.reshape(b * t, d), y_rwkv.reshape(b * t, width),
                                       o.reshape(b * t, width), wts["w_out"], wts["norm2_g"],
                                       wts["rw"], wts["rb"])
    new_k = k.reshape(b, t, width // HEAD_DIM, HEAD_DIM)
    new_v = v.reshape(b, t, width // HEAD_DIM, HEAD_DIM)
    new_shift = u[:, -1, :wts["rwkv_in"]]
    return x1, h2, experts[:, :TOP_K], gates, new_k, new_v, _blockdiag_to_state(h_fin), new_shift


def kernel(x_prompt, x_sample, cache_sb_k, cache_sb_v, state_rwkv, state_shift, norm1_g, w_in, rwkv_mu, rwkv_w0, rwkv_w2, rwkv_a0, rwkv_a2, rwkv_g2, rwkv_k_k, rwkv_k_a, rwkv_r_k, rwkv_ln_g, rwkv_ln_b, sb_q_g, sb_k_g, sb_o_g, w_out, norm2_g, router_w, router_b, moe_w1, moe_b1, moe_w2, moe_b2):
    depth = w_in.shape[0]
    assert depth == 1
    layer = 0
    d = x_prompt.shape[-1]
    width = rwkv_w0.shape[-1]
    rwkv_in = rwkv_mu.shape[-1]
    n_exp = router_w.shape[-1]
    decay_lora, aaa_lora, gate_lora = rwkv_w2.shape[1], rwkv_a2.shape[1], rwkv_g2.shape[1]
    assert decay_lora + aaa_lora == LANES and width % PROJ_TILE == 0
    n_h = width // PROJ_TILE
    n_u = -(-rwkv_in // PROJ_TILE)
    uw = n_u * PROJ_TILE
    tail_w = uw - 3 * width

    w_l = w_in[layer]
    w_all = jnp.concatenate(
        [w_l[:, :rwkv_in], jnp.zeros((d, uw - rwkv_in), F32), w_l[:, rwkv_in:]], axis=1).astype(BF16)
    reps = PROJ_TILE // HEAD_DIM
    row = lambda a: a.reshape(1, -1).astype(F32)
    wts = dict(
        width=width, rwkv_in=rwkv_in, n_u=n_u, n_h=n_h,
        norm1_g=row(norm1_g[layer]), w_all=w_all,
        qg=row(jnp.tile(sb_q_g[layer], reps)), kg=row(jnp.tile(sb_k_g[layer], reps)),
        sb_o_g=row(sb_o_g[layer]),
        mu=jnp.pad(row(rwkv_mu[layer]), ((0, 0), (0, uw - rwkv_in))),
        w0=row(rwkv_w0[layer]), a0=row(rwkv_a0[layer]),
        k_k=row(rwkv_k_k[layer]), k_a=row(rwkv_k_a[layer]),
        w2p=jnp.pad(rwkv_w2[layer], ((0, aaa_lora), (0, 0))).astype(BF16),
        a2p=jnp.pad(rwkv_a2[layer], ((decay_lora, 0), (0, 0))).astype(BF16),
        g2p=jnp.pad(rwkv_g2[layer], ((0, tail_w - LANES - gate_lora), (0, 0))).astype(BF16),
        r_k=row(rwkv_r_k[layer]), ln_g=row(rwkv_ln_g[layer]), ln_b=row(rwkv_ln_b[layer]),
        w_out=w_out[layer].astype(BF16), norm2_g=row(norm2_g[layer]),
        rw=jnp.concatenate(_split_bf16(jnp.pad(router_w[layer], ((0, 0), (0, LANES - n_exp))), 2), axis=1),
        rb=jnp.pad(row(router_b[layer]), ((0, 0), (0, LANES - n_exp)), constant_values=-jnp.inf),
    )

    bp, tp, _ = x_prompt.shape
    bs, ts, _ = x_sample.shape
    heads = width // HEAD_DIM
    s0_prompt = jnp.zeros((bp, heads, HEAD_DIM, HEAD_DIM), F32)
    shift0_prompt = jnp.zeros((bp, rwkv_in), F32)
    grp_p = _mixers(x_prompt, None, s0_prompt, shift0_prompt, wts)
    grp_s = _mixers(x_sample, (cache_sb_k[layer], cache_sb_v[layer]), state_rwkv[layer],
                    state_shift[layer], wts)

    n_p, n_s = bp * tp, bs * ts
    n_tok = n_p + n_s
    h2 = jnp.concatenate([grp_p[1], grp_s[1]], axis=0)
    experts = jnp.concatenate([grp_p[2], grp_s[2]], axis=0)
    n_blocks = -(-(n_tok * TOP_K) // MOE_BLOCK) + n_exp
    dest, block_e, n_valid, n_used = _route(experts, n_exp, n_blocks)
    xbuf, w1b, w2b = _dispatch(h2, dest, n_valid, moe_w1[layer], moe_w2[layer])
    ybuf = _moe_ffn(xbuf, block_e, n_valid, n_used, w1b, moe_b1[layer], w2b, moe_b2[layer])
    y_p = _combine(grp_p[0], grp_p[3], dest[:n_p * TOP_K], ybuf).reshape(bp, tp, d)
    y_s = _combine(grp_s[0], grp_s[3], dest[n_p * TOP_K:], ybuf).reshape(bs, ts, d)

    lead = lambda a: a[None]
    return (y_p, y_s,
            lead(grp_p[4]), lead(grp_p[5]), lead(grp_p[6]), lead(grp_p[7]),
            lead(grp_s[4]), lead(grp_s[5]), lead(grp_s[6]), lead(grp_s[7]))
```

```python
import functools

import jax
import jax.numpy as jnp
from jax import lax
from jax.experimental import pallas as pl
from jax.experimental.pallas import tpu as pltpu

F32 = jnp.float32
BF16 = jnp.bfloat16

LANES = 128
SUBLANES = 8
BF16_SUBLANES = 16
VMEM_LIMIT_BYTES = 56 * 1024 * 1024

HEAD_DIM = 64
HEADS_PER_VREG = LANES // HEAD_DIM
TOP_K = 4
RMS_EPS = 1e-5
GN_EPS = 64e-5
SWIGLU_ALPHA = 1.702
SWIGLU_LIMIT = 7.0
KK_NORM_FLOOR = 1e-12

PROJ_TILE = 512
ROW_TILE = 1024
CHUNK = 64
SB_BLOCK = 128
SB_DEAD_LOGIT = -104.0
SB_PAST_WINDOW = 512
MOE_BLOCK = 512
MOE_FF_TILE = 1024
MOE_ROW_SPLIT = 2
GATHER_TOKENS = 128
DISPATCH_MAX_STEPS = 128


def _cparams(sem):
    return pltpu.CompilerParams(dimension_semantics=sem, vmem_limit_bytes=VMEM_LIMIT_BYTES)


def _sigmoid(x):
    return 1.0 / (1.0 + jnp.exp(-x))


def _softplus(x):
    return jnp.maximum(x, 0.0) + jnp.log(1.0 + jnp.exp(-jnp.abs(x)))


def _split_bf16(x, parts):
    out = []
    rem = x
    for _ in range(parts):
        p = rem.astype(BF16)
        out.append(p)
        rem = rem - p.astype(F32)
    return out


def _dot(a, b):
    return jnp.dot(a, b, preferred_element_type=F32)


def _dot_nt(a, b):
    return lax.dot_general(a, b, (((1,), (1,)), ((), ())), preferred_element_type=F32)


def _dot_tn(a, b):
    return lax.dot_general(a, b, (((0,), (0,)), ((), ())), preferred_element_type=F32)


def _head_ones(n=LANES):
    r = lax.broadcasted_iota(jnp.int32, (n, n), 0) // HEAD_DIM
    c = lax.broadcasted_iota(jnp.int32, (n, n), 1) // HEAD_DIM
    return (r == c).astype(BF16)


def _head_sum(x, ones, parts=1):
    acc = None
    for p in _split_bf16(x, parts):
        t = _dot(p, ones)
        acc = t if acc is None else acc + t
    return acc


def _in_proj_kernel(x_ref, g_ref, w_ref, qg_ref, kg_ref, u_ref, q_ref, k_ref, v_ref, k16_ref, v16_ref,
                    h_scr, *, n_u, n_h):
    j = pl.program_id(1)

    @pl.when(j == 0)
    def _():
        x = x_ref[...]
        ms = jnp.mean(x * x, axis=-1, keepdims=True)
        h_scr[...] = (x * lax.rsqrt(ms + RMS_EPS) * g_ref[...]).astype(BF16)

    acc = _dot(h_scr[...], w_ref[...])

    def head_norm(y, gain):
        ones = _head_ones()
        cols = []
        for c in range(y.shape[1] // LANES):
            yc = y[:, c * LANES:(c + 1) * LANES]
            ms = _head_sum(yc * yc, ones) * (1.0 / HEAD_DIM)
            cols.append(yc * lax.rsqrt(ms + RMS_EPS))
        return jnp.concatenate(cols, axis=1) * gain

    @pl.when(j < n_u)
    def _():
        u_ref[...] = acc

    @pl.when((j >= n_u) & (j < n_u + n_h))
    def _():
        q_ref[...] = head_norm(acc, qg_ref[...])

    @pl.when((j >= n_u + n_h) & (j < n_u + 2 * n_h))
    def _():
        kn = head_norm(acc, kg_ref[...])
        k_ref[...] = kn
        k16_ref[...] = kn.astype(BF16)

    @pl.when(j >= n_u + 2 * n_h)
    def _():
        v_ref[...] = acc
        v16_ref[...] = acc.astype(BF16)


def _in_proj(x2, norm_g, w_all, qg, kg, n_u, n_h):
    n, d = x2.shape
    tm = min(ROW_TILE, n)
    tn = PROJ_TILE
    n_tiles = n_u + 3 * n_h
    assert w_all.shape == (d, n_tiles * tn) and n % tm == 0
    hw = n_h * tn

    def clip_map(lo, cnt):
        return lambda i, j: (i, jnp.clip(j - lo, 0, cnt - 1))

    return pl.pallas_call(
        functools.partial(_in_proj_kernel, n_u=n_u, n_h=n_h),
        grid=(n // tm, n_tiles),
        in_specs=[
            pl.BlockSpec((tm, d), lambda i, j: (i, 0)),
            pl.BlockSpec((1, d), lambda i, j: (0, 0)),
            pl.BlockSpec((d, tn), lambda i, j: (0, j)),
            pl.BlockSpec((1, tn), lambda i, j: (0, 0)),
            pl.BlockSpec((1, tn), lambda i, j: (0, 0)),
        ],
        out_specs=[
            pl.BlockSpec((tm, tn), clip_map(0, n_u)),
            pl.BlockSpec((tm, tn), clip_map(n_u, n_h)),
            pl.BlockSpec((tm, tn), clip_map(n_u + n_h, n_h)),
            pl.BlockSpec((tm, tn), clip_map(n_u + 2 * n_h, n_h)),
            pl.BlockSpec((tm, tn), clip_map(n_u + n_h, n_h)),
            pl.BlockSpec((tm, tn), clip_map(n_u + 2 * n_h, n_h)),
        ],
        out_shape=[
            jax.ShapeDtypeStruct((n, n_u * tn), F32),
            jax.ShapeDtypeStruct((n, hw), F32),
            jax.ShapeDtypeStruct((n, hw), F32),
            jax.ShapeDtypeStruct((n, hw), F32),
            jax.ShapeDtypeStruct((n, hw), BF16),
            jax.ShapeDtypeStruct((n, hw), BF16),
        ],
        scratch_shapes=[pltpu.VMEM((tm, d), BF16)],
        compiler_params=_cparams(("parallel", "arbitrary")),
        name="in_proj",
    )(x2, norm_g, w_all, qg, kg)


def _sb_kernel(*refs, tq, n_past, n_grp):
    if n_past:
        q_ref, k_ref, v_ref, pk_ref, pv_ref, og_ref, o_ref, live_ref, acc_ref = refs
    else:
        q_ref, k_ref, v_ref, og_ref, o_ref, live_ref, acc_ref = refs
        pk_ref = pv_ref = None
    qi = pl.program_id(2)
    n_heads = HEADS_PER_VREG * n_grp
    lane = lax.broadcasted_iota(jnp.int32, (1, LANES), 1)
    in_a = lane < HEAD_DIM
    q_heads = []
    for g in range(n_grp):
        q = q_ref[0, :, g * LANES:(g + 1) * LANES] * (HEAD_DIM ** -0.5)
        q_heads.append(jnp.where(in_a, q, 0.0).astype(BF16))
        q_heads.append(jnp.where(in_a, 0.0, q).astype(BF16))
    acc_ref[...] = jnp.zeros_like(acc_ref)

    def visit(k_src, v_src, r0, tk, carries, mask):
        rows = pl.ds(r0, tk)
        s_idx = lax.broadcasted_iota(jnp.int32, (tk, tk), 0)
        j_idx = lax.broadcasted_iota(jnp.int32, (tk, tk), 1)
        suffix = (s_idx >= j_idx).astype(BF16)
        kbs = [k_src[0, rows, g * LANES:(g + 1) * LANES] for g in range(n_grp)]
        zs = [_dot_nt(q_heads[h], kbs[h // HEADS_PER_VREG]) for h in range(n_heads)]
        splits = []
        for z in zs:
            log_1mb = -_softplus(z)
            if mask is not None:
                log_1mb = jnp.where(mask, log_1mb, 0.0)
            splits.append(_split_bf16(log_1mb, 2))
        incls = [_dot(hi, suffix) + _dot(lo, suffix) for hi, lo in splits]
        ps = []
        for z, incl, carry in zip(zs, incls, carries):
            p = jnp.exp(z + incl + carry)
            if mask is not None:
                p = jnp.where(mask, p, 0.0)
            ps.append(p.astype(BF16))
        vbs = [v_src[0, rows, g * LANES:(g + 1) * LANES] for g in range(n_grp)]
        for h in range(n_heads):
            acc_ref[h] += _dot(ps[h], vbs[h // HEADS_PER_VREG])
        return tuple(c + incl[:, 0:1] for c, incl in zip(carries, incls))

    q0 = pl.multiple_of(qi * tq, tq)
    t_idx = lax.broadcasted_iota(jnp.int32, (tq, tq), 0)
    j_idx = lax.broadcasted_iota(jnp.int32, (tq, tq), 1)
    zero = jnp.zeros((tq, 1), F32)
    carries = visit(k_ref, v_ref, q0, tq, (zero,) * n_heads, j_idx < t_idx)

    def alive(carries):
        top = carries[0]
        for c in carries[1:]:
            top = jnp.maximum(top, c)
        return (jnp.max(top) > SB_DEAD_LOGIT).astype(jnp.int32)

    def sweep(kr, vr, n_blocks, carries):
        def cond(st):
            return (st[0] >= 0) & (st[1] > 0)

        def body(st):
            b = st[0]
            r0 = pl.multiple_of(b * SB_BLOCK, SB_BLOCK)
            c = visit(kr, vr, r0, SB_BLOCK, st[2:], None)
            return (b - 1, alive(c)) + c

        st = lax.while_loop(cond, body, (n_blocks - 1, alive(carries)) + carries)
        return st[2:]

    if tq == SB_BLOCK:
        carries = sweep(k_ref, v_ref, qi, carries)
    if n_past:
        carries = sweep(pk_ref, pv_ref, jnp.int32(n_past), carries)
    live_ref[...] = jnp.broadcast_to(alive(carries), live_ref.shape)

    ones = _head_ones()
    for g in range(n_grp):
        o = jnp.where(in_a, acc_ref[HEADS_PER_VREG * g], acc_ref[HEADS_PER_VREG * g + 1])
        ms = _head_sum(o * o, ones) * (1.0 / HEAD_DIM)
        cs = slice(g * LANES, (g + 1) * LANES)
        o_ref[0, :, cs] = o * lax.rsqrt(ms + RMS_EPS) * og_ref[:, cs]


def _sb_attend(q, k, v, og, past_k=None, past_v=None):
    b, t, w = q.shape
    n_grp = 4
    gw = n_grp * LANES
    assert w % gw == 0
    tq = min(SB_BLOCK, t)
    assert t % tq == 0 and (t == tq or tq == SB_BLOCK)
    n_past = 0
    in_specs = [
        pl.BlockSpec((1, tq, gw), lambda bi, p, qi: (bi, qi, p)),
        pl.BlockSpec((1, t, gw), lambda bi, p, qi: (bi, 0, p)),
        pl.BlockSpec((1, t, gw), lambda bi, p, qi: (bi, 0, p)),
    ]
    args = [q, k, v]
    if past_k is not None:
        tp = past_k.shape[1]
        assert tp % SB_BLOCK == 0 and t == tq
        n_past = tp // SB_BLOCK
        in_specs += [pl.BlockSpec((1, tp, gw), lambda bi, p, qi: (bi, 0, p))] * 2
        args += [past_k, past_v]
    in_specs.append(pl.BlockSpec((1, gw), lambda bi, p, qi: (0, p)))
    args.append(og)
    return pl.pallas_call(
        functools.partial(_sb_kernel, tq=tq, n_past=n_past, n_grp=n_grp),
        grid=(b, w // gw, t // tq),
        in_specs=in_specs,
        out_specs=[pl.BlockSpec((1, tq, gw), lambda bi, p, qi: (bi, qi, p)),
                   pl.BlockSpec((1, 1, 1, SUBLANES, LANES), lambda bi, p, qi: (bi, p, qi, 0, 0))],
        out_shape=[jax.ShapeDtypeStruct((b, t, w), F32),
                   jax.ShapeDtypeStruct((b, w // gw, t // tq, SUBLANES, LANES), jnp.int32)],
        scratch_shapes=[pltpu.VMEM((HEADS_PER_VREG * n_grp, tq, LANES), F32)],
        compiler_params=_cparams(("parallel", "parallel", "arbitrary")),
        name="sb_attend",
    )(*args)


def _rwkv_prep_kernel(u_ref, sh_ref, mu_ref, w0_ref, a0_ref, kk_ref, ka_ref, w2_ref, a2_ref, g2_ref,
                      r_o, lw_o, kf_o, v_o, nkk_o, b_o, g_o, prev_scr, *, width):
    ti = pl.program_id(1)
    tt = u_ref.shape[1]

    @pl.when(ti == 0)
    def _():
        prev_scr[...] = sh_ref[0]

    row = lax.broadcasted_iota(jnp.int32, (tt, 1), 0)

    def mixed(lo, hi):
        u_raw = u_ref[0, :, lo:hi]
        shifted = jnp.where(row == 0, prev_scr[:, lo:hi], pltpu.roll(u_raw, 1, axis=0))
        return u_raw + (shifted - u_raw) * mu_ref[:, lo:hi]

    r = mixed(0, width)
    k = mixed(width, 2 * width)
    v = mixed(2 * width, 3 * width)
    tail = mixed(3 * width, u_ref.shape[2])
    prev_scr[...] = u_ref[0, tt - 1:tt, :]

    lora_in = tail[:, :LANES]
    dw = _dot(jnp.tanh(lora_in).astype(BF16), w2_ref[...])
    da = _dot(lora_in.astype(BF16), a2_ref[...])
    g = _dot(_sigmoid(tail[:, LANES:]).astype(BF16), g2_ref[...])
    w_log = -_softplus(-(w0_ref[...] + dw)) - 0.5
    a = _sigmoid(a0_ref[...] + da)

    kk = k * kk_ref[...]
    ones = _head_ones()
    cols = []
    for c in range(width // LANES):
        kc = kk[:, c * LANES:(c + 1) * LANES]
        norm = jnp.sqrt(_head_sum(kc * kc, ones))
        cols.append(kc / jnp.maximum(norm, KK_NORM_FLOOR))
    kk = jnp.concatenate(cols, axis=1)

    r_o[0] = r
    lw_o[0] = -jnp.exp(w_log)
    kf_o[0] = k * (1.0 + (a - 1.0) * ka_ref[...])
    v_o[0] = v
    nkk_o[0] = -kk
    b_o[0] = kk * a
    g_o[0] = g


def _rwkv_prep(u, shift_prev, mu, w0, a0, k_k, k_a, w2p, a2p, g2p, width):
    b, t, uw = u.shape
    tt = min(256, t)
    assert t % tt == 0
    row_spec = pl.BlockSpec((1, width), lambda bi, ti: (0, 0))
    out_spec = pl.BlockSpec((1, tt, width), lambda bi, ti: (bi, ti, 0))
    return pl.pallas_call(
        functools.partial(_rwkv_prep_kernel, width=width),
        grid=(b, t // tt),
        in_specs=[
            pl.BlockSpec((1, tt, uw), lambda bi, ti: (bi, ti, 0)),
            pl.BlockSpec((1, 1, uw), lambda bi, ti: (bi, 0, 0)),
            pl.BlockSpec((1, uw), lambda bi, ti: (0, 0)),
            row_spec, row_spec, row_spec, row_spec,
            pl.BlockSpec(w2p.shape, lambda bi, ti: (0, 0)),
            pl.BlockSpec(a2p.shape, lambda bi, ti: (0, 0)),
            pl.BlockSpec(g2p.shape, lambda bi, ti: (0, 0)),
        ],
        out_specs=[out_spec] * 7,
        out_shape=[jax.ShapeDtypeStruct((b, t, width), F32)] * 7,
        scratch_shapes=[pltpu.VMEM((1, uw), F32)],
        compiler_params=_cparams(("parallel", "arbitrary")),
        name="rwkv_prep",
    )(u, shift_prev, mu, w0, a0, k_k, k_a, w2p, a2p, g2p)


def _rwkv_chunk_kernel(r_ref, lw_ref, k_ref, v_ref, nkk_ref, b_ref, g_ref, h0_ref,
                       rk_ref, lng_ref, lnb_ref, y_ref, hout_ref, h_scr, *, n_bat, n_pairs, n_chunks):
    ti = pl.program_id(1)
    c_len = CHUNK

    @pl.when(ti == 0)
    def _():
        h_scr[...] = h0_ref[...]

    lane = lax.broadcasted_iota(jnp.int32, (1, LANES), 1)
    head_lanes = (lane < HEAD_DIM, lane >= HEAD_DIM)
    t_idx = lax.broadcasted_iota(jnp.int32, (c_len, c_len), 0)
    i_idx = lax.broadcasted_iota(jnp.int32, (c_len, c_len), 1)
    strict = i_idx < t_idx
    causal = i_idx <= t_idx
    prefix = causal.astype(BF16)
    eye_c = (i_idx == t_idx).astype(F32)
    r128 = lax.broadcasted_iota(jnp.int32, (LANES, LANES), 0)
    c128 = lax.broadcasted_iota(jnp.int32, (LANES, LANES), 1)
    same_head = (r128 // HEAD_DIM) == (c128 // HEAD_DIM)
    eye128 = r128 == c128
    ones = _head_ones()
    n_sub = HEADS_PER_VREG

    def chunk_group(bi, r0):
        rows = pl.ds(r0, c_len)
        pairs = range(n_pairs)
        cols = [slice(pp * LANES, (pp + 1) * LANES) for pp in pairs]
        heads = [(pp, s) for pp in pairs for s in range(n_sub)]

        cums = []
        for pp in pairs:
            parts = _split_bf16(lw_ref[bi, rows, cols[pp]], 3)
            cums.append(_dot(prefix, parts[0]) + _dot(prefix, parts[1]) + _dot(prefix, parts[2]))

        a_ts, r_ts, b16s, k16s, v16s, bks, p_ends = [], [], [], [], [], [], []
        for pp in pairs:
            cs = cols[pp]
            cum = cums[pp]
            lw = lw_ref[bi, rows, cs]
            kf = k_ref[bi, rows, cs]
            bb = b_ref[bi, rows, cs]
            p_inv = jnp.exp(-cum)
            p_end = cum[c_len - 1:c_len, :]
            to_end = jnp.exp(p_end - cum)
            a_ts.append(nkk_ref[bi, rows, cs] * jnp.exp(cum - lw))
            r_ts.append(r_ref[bi, rows, cs] * jnp.exp(cum))
            b16s.append((bb * p_inv).astype(BF16))
            k16s.append((kf * p_inv).astype(BF16))
            v16s.append(v_ref[bi, rows, cs].astype(BF16))
            bks.append(jnp.concatenate([bb * to_end, kf * to_end], axis=0).astype(BF16))
            p_ends.append(p_end)

        m_bs, m_ks = [], []
        for pp, s in heads:
            sel = head_lanes[s]
            lhs = jnp.concatenate([jnp.where(sel, a_ts[pp], 0.0), jnp.where(sel, r_ts[pp], 0.0)],
                                  axis=0).astype(BF16)
            m_bs.append(_dot_nt(lhs, b16s[pp]))
            m_ks.append(_dot_nt(lhs, k16s[pp]))

        a_rbs = [jnp.where(causal, m[c_len:], 0.0).astype(BF16) for m in m_bs]
        a_rks = [jnp.where(causal, m[c_len:], 0.0).astype(BF16) for m in m_ks]
        avs = [_dot(jnp.where(strict, m[:c_len], 0.0).astype(BF16), v16s[pp])
               for m, (pp, _) in zip(m_ks, heads)]
        n_pows = [jnp.where(strict, m[:c_len], 0.0) for m in m_bs]
        t_invs = [eye_c + n for n in n_pows]
        span = 2
        while span < c_len:
            n16s = [n.astype(BF16) for n in n_pows]
            n_pows = [_dot(n, n) for n in n16s]
            t_invs = [_dot(t.astype(BF16), (eye_c + n).astype(BF16)) for t, n in zip(t_invs, n_pows)]
            span *= 2

        wus = [_dot(t.astype(BF16), jnp.concatenate([a_ts[pp], av], axis=1).astype(BF16))
               for t, av, (pp, _) in zip(t_invs, avs, heads)]
        zqs = [_dot(a, wu.astype(BF16)) for a, wu in zip(a_rbs, wus)]
        yks = [_dot(a, v16s[pp]) for a, (pp, _) in zip(a_rks, heads)]

        ys, gjs = [], []
        for pp in pairs:
            h0, h1 = n_sub * pp, n_sub * pp + 1
            sel_a = head_lanes[0]
            w2 = jnp.where(sel_a, wus[h0][:, :LANES], wus[h1][:, :LANES])
            u0 = jnp.where(sel_a, wus[h0][:, LANES:], wus[h1][:, LANES:])
            q2 = r_ts[pp] + jnp.where(sel_a, zqs[h0][:, :LANES], zqs[h1][:, :LANES])
            y0 = jnp.where(sel_a, zqs[h0][:, LANES:] + yks[h0], zqs[h1][:, LANES:] + yks[h1])
            h16 = h_scr[bi, pp].astype(BF16)
            ys.append(_dot(q2.astype(BF16), h16) + y0)
            top = jnp.concatenate([w2, u0], axis=1).astype(BF16)
            bot = jnp.concatenate([jnp.zeros((c_len, LANES), BF16), v16s[pp]], axis=1)
            gjs.append(_dot_tn(bks[pp], jnp.concatenate([top, bot], axis=0)))

        for pp in pairs:
            gj = gjs[pp]
            g_mat = jnp.where(eye128, jnp.exp(p_ends[pp]), 0.0) + jnp.where(same_head, gj[:, :LANES], 0.0)
            h16 = h_scr[bi, pp].astype(BF16)
            h_scr[bi, pp] = _dot(g_mat.astype(BF16), h16) + jnp.where(same_head, gj[:, LANES:], 0.0)

        means = [_head_sum(y, ones, parts=2) * (1.0 / HEAD_DIM) for y in ys]
        ycs = [y - m for y, m in zip(ys, means)]
        variances = [_head_sum(yc * yc, ones) * (1.0 / HEAD_DIM) for yc in ycs]
        bonuses = [_head_sum(r_ref[bi, rows, cols[pp]] * k_ref[bi, rows, cols[pp]] * rk_ref[:, cols[pp]], ones)
                   for pp in pairs]
        for pp in pairs:
            cs = cols[pp]
            yn = ycs[pp] * lax.rsqrt(variances[pp] + GN_EPS) * lng_ref[:, cs] + lnb_ref[:, cs]
            y_ref[bi, rows, cs] = (yn + bonuses[pp] * v_ref[bi, rows, cs]) * g_ref[bi, rows, cs]

    def chunk_step(ci, carry):
        r0 = pl.multiple_of(ci * c_len, c_len)
        for bi in range(n_bat):
            chunk_group(bi, r0)
        return carry

    lax.fori_loop(0, n_chunks, chunk_step, 0)

    @pl.when(ti == pl.num_programs(1) - 1)
    def _():
        hout_ref[...] = h_scr[...]


def _rwkv_chunk(r, lw, kf, v, nkk, bb, g, h0, r_k, ln_g, ln_b):
    b, t, w = r.shape
    pairs = w // LANES
    tt = min(2 * CHUNK, t)
    n_bat = 2 if b % 2 == 0 else 1
    assert t % tt == 0 and tt % CHUNK == 0 and b % n_bat == 0
    seq = pl.BlockSpec((n_bat, tt, w), lambda bi, ti: (bi, ti, 0))
    par = pl.BlockSpec((1, w), lambda bi, ti: (0, 0))
    st = pl.BlockSpec((n_bat, pairs, LANES, LANES), lambda bi, ti: (bi, 0, 0, 0))
    return pl.pallas_call(
        functools.partial(_rwkv_chunk_kernel, n_bat=n_bat, n_pairs=pairs, n_chunks=tt // CHUNK),
        grid=(b // n_bat, t // tt),
        in_specs=[seq] * 7 + [st, par, par, par],
        out_specs=[seq, st],
        out_shape=[jax.ShapeDtypeStruct((b, t, w), F32),
                   jax.ShapeDtypeStruct((b, pairs, LANES, LANES), F32)],
        scratch_shapes=[pltpu.VMEM((n_bat, pairs, LANES, LANES), F32)],
        compiler_params=_cparams(("parallel", "arbitrary")),
        name="rwkv_chunk",
    )(r, lw, kf, v, nkk, bb, g, h0, r_k, ln_g, ln_b)


def _state_to_blockdiag(s):
    b, h, dv, dk = s.shape
    st = jnp.swapaxes(s, -1, -2).reshape(b, h // 2, 2, dk, dv)
    z = jnp.zeros_like(st[:, :, 0])
    top = jnp.concatenate([st[:, :, 0], z], axis=-1)
    bot = jnp.concatenate([z, st[:, :, 1]], axis=-1)
    return jnp.concatenate([top, bot], axis=-2)


def _blockdiag_to_state(hm):
    b, p, _, _ = hm.shape
    a = hm[:, :, :HEAD_DIM, :HEAD_DIM]
    c = hm[:, :, HEAD_DIM:, HEAD_DIM:]
    st = jnp.stack([a, c], axis=2).reshape(b, 2 * p, HEAD_DIM, HEAD_DIM)
    return jnp.swapaxes(st, -1, -2)


def _out_proj_kernel(x_ref, yr_ref, o_ref, w_ref, g_ref, rw_ref, rb_ref,
                     x1_ref, h2_ref, e_ref, gate_ref):
    half = yr_ref.shape[1]
    x1 = (x_ref[...] + _dot(yr_ref[...].astype(BF16), w_ref[:half, :])
          + _dot(o_ref[...].astype(BF16), w_ref[half:, :]))
    x1_ref[...] = x1
    ms = jnp.mean(x1 * x1, axis=-1, keepdims=True)
    h2 = x1 * lax.rsqrt(ms + RMS_EPS) * g_ref[...]
    h2_ref[...] = h2
    h_hi, h_lo = _split_bf16(h2, 2)
    both = _dot(h_hi, rw_ref[...])
    logits = both[:, :LANES] + both[:, LANES:] + _dot(h_lo, rw_ref[:, :LANES]) + rb_ref[...]
    lane = lax.broadcasted_iota(jnp.int32, logits.shape, 1)
    lane_f = lane.astype(F32)
    experts = jnp.zeros(logits.shape, F32)
    tops = jnp.zeros(logits.shape, F32)
    work = logits
    top0 = None
    for kth in range(TOP_K):
        m = jnp.max(work, axis=-1, keepdims=True)
        idx = jnp.min(jnp.where(work == m, lane_f, float(LANES)), axis=-1, keepdims=True)
        if top0 is None:
            top0 = m
        experts = jnp.where(lane == kth, idx, experts)
        tops = jnp.where(lane == kth, jnp.exp(m - top0), tops)
        work = jnp.where(lane_f == idx, -jnp.inf, work)
    e_ref[...] = experts.astype(jnp.int32)
    gate_ref[...] = tops / jnp.sum(tops, axis=-1, keepdims=True)


def _out_proj(x2, yr, o, w_out, norm_g, rw, rb):
    n, d = x2.shape
    tm = min(256, n)
    assert n % tm == 0
    half = yr.shape[1]
    row = lambda i: (i, 0)
    fixed = lambda i: (0, 0)
    return pl.pallas_call(
        _out_proj_kernel,
        grid=(n // tm,),
        in_specs=[
            pl.BlockSpec((tm, d), row),
            pl.BlockSpec((tm, half), row),
            pl.BlockSpec((tm, half), row),
            pl.BlockSpec(w_out.shape, fixed),
            pl.BlockSpec((1, d), fixed),
            pl.BlockSpec(rw.shape, fixed),
            pl.BlockSpec((1, LANES), fixed),
        ],
        out_specs=[
            pl.BlockSpec((tm, d), row),
            pl.BlockSpec((tm, d), row),
            pl.BlockSpec((tm, LANES), row),
            pl.BlockSpec((tm, LANES), row),
        ],
        out_shape=[
            jax.ShapeDtypeStruct((n, d), F32),
            jax.ShapeDtypeStruct((n, d), F32),
            jax.ShapeDtypeStruct((n, LANES), jnp.int32),
            jax.ShapeDtypeStruct((n, LANES), F32),
        ],
        compiler_params=_cparams(("parallel",)),
        name="out_proj",
    )(x2, yr, o, w_out, norm_g, rw, rb)


def _dispatch_kernel(nv_ref, dest_ref, h2_ref, w1_ref, w2_ref, xbuf_hbm, w1b_ref, w2b_ref, zero_scr, sem,
                     *, tokens, n_blocks):
    w1b_ref[...] = w1_ref[...].astype(BF16)
    w2b_ref[...] = w2_ref[...].astype(BF16)

    @pl.when(pl.program_id(0) == 0)
    def _():
        zero_scr[...] = jnp.zeros_like(zero_scr)

        def fill(i):
            r0 = pl.multiple_of(i * MOE_BLOCK, MOE_BLOCK)
            return pltpu.make_async_copy(zero_scr, xbuf_hbm.at[pl.ds(r0, MOE_BLOCK)], sem)

        def fill_start(i, c):
            @pl.when(nv_ref[i] < MOE_BLOCK)
            def _():
                fill(i).start()
            return c

        def fill_wait(i, c):
            @pl.when(nv_ref[i] < MOE_BLOCK)
            def _():
                fill(i).wait()
            return c

        lax.fori_loop(0, n_blocks, fill_start, 0)
        lax.fori_loop(0, n_blocks, fill_wait, 0)

    def copy(t0, t, k):
        return pltpu.make_async_copy(h2_ref.at[pl.ds(t0 + t, 1)],
                                     xbuf_hbm.at[pl.ds(dest_ref[0, 0, (t0 + t) * TOP_K + k], 1)], sem)

    def for_rows(fn):
        def group(g, c):
            t0 = pl.multiple_of(g * SUBLANES, SUBLANES)
            for t in range(SUBLANES):
                for k in range(TOP_K):
                    fn(copy(t0, t, k))
            return c
        lax.fori_loop(0, tokens // SUBLANES, group, 0)

    for_rows(lambda c: c.start())
    for_rows(lambda c: c.wait())


def _dispatch(h2, dest, n_valid, w1, w2):
    n, d = h2.shape
    steps = 1
    while steps < DISPATCH_MAX_STEPS and n % (2 * steps * SUBLANES) == 0:
        steps *= 2
    tokens = n // steps
    rows = tokens * TOP_K
    n_blocks = n_valid.shape[0]
    w1f = w1.reshape(-1, w1.shape[-1])
    w2f = w2.reshape(-1, w2.shape[-1])
    r1, r2 = w1f.shape[0] // steps, w2f.shape[0] // steps
    assert w1f.shape[0] % steps == 0 and w2f.shape[0] % steps == 0
    assert r1 % BF16_SUBLANES == 0 and r2 % BF16_SUBLANES == 0
    grid_spec = pltpu.PrefetchScalarGridSpec(
        num_scalar_prefetch=1,
        grid=(steps,),
        in_specs=[
            pl.BlockSpec((1, 1, rows), lambda i, nv: (i, 0, 0), memory_space=pltpu.SMEM),
            pl.BlockSpec((tokens, d), lambda i, nv: (i, 0)),
            pl.BlockSpec((r1, w1f.shape[1]), lambda i, nv: (i, 0)),
            pl.BlockSpec((r2, w2f.shape[1]), lambda i, nv: (i, 0)),
        ],
        out_specs=[
            pl.BlockSpec(memory_space=pl.ANY),
            pl.BlockSpec((r1, w1f.shape[1]), lambda i, nv: (i, 0)),
            pl.BlockSpec((r2, w2f.shape[1]), lambda i, nv: (i, 0)),
        ],
        scratch_shapes=[pltpu.VMEM((MOE_BLOCK, d), F32), pltpu.SemaphoreType.DMA],
    )
    xbuf, w1b, w2b = pl.pallas_call(
        functools.partial(_dispatch_kernel, tokens=tokens, n_blocks=n_blocks),
        grid_spec=grid_spec,
        out_shape=[jax.ShapeDtypeStruct((n_blocks * MOE_BLOCK, d), F32),
                   jax.ShapeDtypeStruct(w1f.shape, BF16),
                   jax.ShapeDtypeStruct(w2f.shape, BF16)],
        compiler_params=_cparams(("arbitrary",)),
        name="moe_dispatch",
    )(n_valid, dest.reshape(steps, 1, rows), h2, w1f, w2f)
    return xbuf, w1b.reshape(w1.shape), w2b.reshape(w2.shape)


def _moe_ffn_kernel(be_ref, nv_ref, nu_ref, x_ref, w1g_ref, w1l_ref, b1g_ref, b1l_ref, w2_ref, b2_ref,
                    y_ref, xs_scr):
    i = pl.program_id(0)
    f = pl.program_id(1)
    n_valid = nv_ref[i]

    @pl.when((n_valid == 0) & (f == 0))
    def _():
        y_ref[...] = jnp.zeros_like(y_ref)

    @pl.when(n_valid > 0)
    def _():
        @pl.when(f == 0)
        def _():
            xs_scr[...] = x_ref[...].astype(BF16)
            y_ref[...] = jnp.broadcast_to(b2_ref[0], y_ref.shape)

        bm = y_ref.shape[0]
        halves = [pl.ds(s * (bm // MOE_ROW_SPLIT), bm // MOE_ROW_SPLIT) for s in range(MOE_ROW_SPLIT)]
        xs = [xs_scr[rows, :] for rows in halves]
        hgs = [_dot(x, w1g_ref[0]) + b1g_ref[0] for x in xs]
        hls = [_dot(x, w1l_ref[0]) + b1l_ref[0] for x in xs]
        acts = []
        for hg, hl in zip(hgs, hls):
            glu = jnp.minimum(hg, SWIGLU_LIMIT)
            lin = jnp.clip(hl, -SWIGLU_LIMIT, SWIGLU_LIMIT)
            acts.append((glu * _sigmoid(SWIGLU_ALPHA * glu) * (lin + 1.0)).astype(BF16))
        parts = [_dot(act, w2_ref[0]) for act in acts]
        for rows, part in zip(halves, parts):
            y_ref[rows, :] += part


def _moe_ffn(xbuf, block_e, n_valid, n_used, w1, b1, w2, b2):
    n_rows, d = xbuf.shape
    n_exp, _, ff2 = w1.shape
    ff = ff2 // 2
    bm, tf = MOE_BLOCK, MOE_FF_TILE
    n_f = ff // tf
    n_blocks = n_rows // bm
    b1r = b1.reshape(n_exp, 1, ff2)
    b2r = b2.reshape(n_exp, 1, d)
    def ff(i, f, nv):
        return jnp.where(nv[i] > 0, f, n_f - 1)

    def xi(i, nv, nu):
        return jnp.minimum(i, nu[0] - 1)

    grid_spec = pltpu.PrefetchScalarGridSpec(
        num_scalar_prefetch=3,
        grid=(n_blocks, n_f),
        in_specs=[
            pl.BlockSpec((bm, d), lambda i, f, be, nv, nu: (xi(i, nv, nu), 0)),
            pl.BlockSpec((1, d, tf), lambda i, f, be, nv, nu: (be[i], 0, ff(i, f, nv))),
            pl.BlockSpec((1, d, tf), lambda i, f, be, nv, nu: (be[i], 0, n_f + ff(i, f, nv))),
            pl.BlockSpec((1, 1, tf), lambda i, f, be, nv, nu: (be[i], 0, ff(i, f, nv))),
            pl.BlockSpec((1, 1, tf), lambda i, f, be, nv, nu: (be[i], 0, n_f + ff(i, f, nv))),
            pl.BlockSpec((1, tf, d), lambda i, f, be, nv, nu: (be[i], ff(i, f, nv), 0)),
            pl.BlockSpec((1, 1, d), lambda i, f, be, nv, nu: (be[i], 0, 0)),
        ],
        out_specs=pl.BlockSpec((bm, d), lambda i, f, be, nv, nu: (i, 0)),
        scratch_shapes=[pltpu.VMEM((bm, d), BF16)],
    )
    return pl.pallas_call(
        _moe_ffn_kernel,
        grid_spec=grid_spec,
        out_shape=jax.ShapeDtypeStruct((n_rows, d), F32),
        compiler_params=_cparams(("parallel", "arbitrary")),
        name="moe_ffn",
    )(block_e, n_valid, n_used, xbuf, w1, w1, b1r, b1r, w2, b2r)


def _combine_kernel(dest_ref, next_ref, x1_ref, gate_ref, ybuf_hbm, out_ref, rows_scr, sems, *, tokens):
    i = pl.program_id(0)
    n = pl.num_programs(0)
    slot = i % 2

    def copy(idx_ref, buf, t0, t, k):
        return pltpu.make_async_copy(ybuf_hbm.at[pl.ds(idx_ref[0, 0, (t0 + t) * TOP_K + k], 1)],
                                     rows_scr.at[buf, k, pl.ds(t0 + t, 1)], sems.at[buf])

    def for_rows(idx_ref, buf, fn):
        def group(g, c):
            t0 = pl.multiple_of(g * SUBLANES, SUBLANES)
            for t in range(SUBLANES):
                for k in range(TOP_K):
                    fn(copy(idx_ref, buf, t0, t, k))
            return c
        lax.fori_loop(0, tokens // SUBLANES, group, 0)

    def start_all(idx_ref, buf):
        for_rows(idx_ref, buf, lambda c: c.start())

    @pl.when(i == 0)
    def _():
        start_all(dest_ref, 0)

    @pl.when(i + 1 < n)
    def _():
        start_all(next_ref, 1 - slot)

    for_rows(dest_ref, slot, lambda c: c.wait())
    acc = x1_ref[...]
    gate = gate_ref[...]
    for kth in range(TOP_K):
        acc = acc + gate[:, kth:kth + 1] * rows_scr[slot, kth]
    out_ref[...] = acc


def _combine(x1, gate, dest, ybuf):
    n, d = x1.shape
    tokens = min(GATHER_TOKENS, n)
    assert n % tokens == 0
    steps = n // tokens
    idx = dest.reshape(steps, 1, tokens * TOP_K)
    idx_block = (1, 1, tokens * TOP_K)
    return pl.pallas_call(
        functools.partial(_combine_kernel, tokens=tokens),
        grid=(steps,),
        in_specs=[
            pl.BlockSpec(idx_block, lambda i: (i, 0, 0), memory_space=pltpu.SMEM),
            pl.BlockSpec(idx_block, lambda i: (jnp.minimum(i + 1, steps - 1), 0, 0), memory_space=pltpu.SMEM),
            pl.BlockSpec((tokens, d), lambda i: (i, 0)),
            pl.BlockSpec((tokens, LANES), lambda i: (i, 0)),
            pl.BlockSpec(memory_space=pl.ANY),
        ],
        out_specs=pl.BlockSpec((tokens, d), lambda i: (i, 0)),
        out_shape=jax.ShapeDtypeStruct((n, d), F32),
        scratch_shapes=[pltpu.VMEM((2, TOP_K, tokens, d), F32), pltpu.SemaphoreType.DMA((2,))],
        compiler_params=_cparams(("arbitrary",)),
        name="moe_combine",
    )(idx, idx, x1, gate, ybuf)


def _route(experts, n_exp, n_blocks):
    flat_e = experts.reshape(-1)
    onehot = (flat_e[:, None] == jnp.arange(n_exp, dtype=jnp.int32)[None, :]).astype(jnp.int32)
    csum = jnp.cumsum(onehot, axis=0)
    counts = csum[-1]
    padded = (counts + MOE_BLOCK - 1) // MOE_BLOCK * MOE_BLOCK
    pend = jnp.cumsum(padded)
    pstart = pend - padded
    dest = jnp.sum(onehot * (csum - 1 + pstart[None, :]), axis=1)
    block_start = jnp.arange(n_blocks, dtype=jnp.int32) * MOE_BLOCK
    block_e = jnp.minimum(jnp.sum(block_start[:, None] >= pend[None, :], axis=1), n_exp - 1)
    n_valid = jnp.clip(counts[block_e] - (block_start - pstart[block_e]), 0, MOE_BLOCK)
    n_used = jnp.maximum(pend[-1] // MOE_BLOCK, 1)
    block_e = block_e[jnp.minimum(jnp.arange(n_blocks), n_used - 1)]
    i32 = lambda a: a.astype(jnp.int32)
    return i32(dest), i32(block_e), i32(n_valid), i32(n_used).reshape(1)


def _mixers(x, cache, s0, shift_prev, wts):
    b, t, d = x.shape
    width = wts["width"]
    u, q, k, v, k16, v16 = _in_proj(x.reshape(b * t, d), wts["norm1_g"], wts["w_all"], wts["qg"],
                                    wts["kg"], wts["n_u"], wts["n_h"])
    uw = u.shape[1]
    u = u.reshape(b, t, uw)
    q, k, v, k16, v16 = (a.reshape(b, t, width) for a in (q, k, v, k16, v16))
    if cache is None:
        o, _ = _sb_attend(q, k16, v16, wts["sb_o_g"])
    else:
        def past(rows):
            return tuple(c[:, c.shape[1] - rows:].astype(BF16).reshape(b, rows, width) for c in cache)

        n_rows = cache[0].shape[1]
        if n_rows <= SB_PAST_WINDOW:
            o, _ = _sb_attend(q, k16, v16, wts["sb_o_g"], *past(n_rows))
        else:
            o, live = _sb_attend(q, k16, v16, wts["sb_o_g"], *past(SB_PAST_WINDOW))
            o = lax.cond(jnp.any(live > 0),
                         lambda: _sb_attend(q, k16, v16, wts["sb_o_g"], *past(n_rows))[0],
                         lambda: o)

    shift_p = jnp.pad(shift_prev, ((0, 0), (0, uw - shift_prev.shape[1]))).reshape(b, 1, uw)
    r, lw, kf, vv, nkk, bb, g = _rwkv_prep(u, shift_p, wts["mu"], wts["w0"], wts["a0"], wts["k_k"],
                                           wts["k_a"], wts["w2p"], wts["a2p"], wts["g2p"], width)
    y_rwkv, h_fin = _rwkv_chunk(r, lw, kf, vv, nkk, bb, g, _state_to_blockdiag(s0),
                                wts["r_k"], wts["ln_g"], wts["ln_b"])
    x1, h2, experts, gates = _out_proj(x.reshape(b * t, d), y_rwkv.reshape(b * t, width),
                                       o.reshape(b * t, width), wts["w_out"], wts["norm2_g"],
                                       wts["rw"], wts["rb"])
    new_k = k.reshape(b, t, width // HEAD_DIM, HEAD_DIM)
    new_v = v.reshape(b, t, width // HEAD_DIM, HEAD_DIM)
    new_shift = u[:, -1, :wts["rwkv_in"]]
    return x1, h2, experts[:, :TOP_K], gates, new_k, new_v, _blockdiag_to_state(h_fin), new_shift


def kernel(x_prompt, x_sample, cache_sb_k, cache_sb_v, state_rwkv, state_shift, norm1_g, w_in, rwkv_mu, rwkv_w0, rwkv_w2, rwkv_a0, rwkv_a2, rwkv_g2, rwkv_k_k, rwkv_k_a, rwkv_r_k, rwkv_ln_g, rwkv_ln_b, sb_q_g, sb_k_g, sb_o_g, w_out, norm2_g, router_w, router_b, moe_w1, moe_b1, moe_w2, moe_b2):
    depth = w_in.shape[0]
    assert depth == 1
    layer = 0
    d = x_prompt.shape[-1]
    width = rwkv_w0.shape[-1]
    rwkv_in = rwkv_mu.shape[-1]
    n_exp = router_w.shape[-1]
    decay_lora, aaa_lora, gate_lora = rwkv_w2.shape[1], rwkv_a2.shape[1], rwkv_g2.shape[1]
    assert decay_lora + aaa_lora == LANES and width % PROJ_TILE == 0
    n_h = width // PROJ_TILE
    n_u = -(-rwkv_in // PROJ_TILE)
    uw = n_u * PROJ_TILE
    tail_w = uw - 3 * width

    w_l = w_in[layer]
    w_all = jnp.concatenate(
        [w_l[:, :rwkv_in], jnp.zeros((d, uw - rwkv_in), F32), w_l[:, rwkv_in:]], axis=1).astype(BF16)
    reps = PROJ_TILE // HEAD_DIM
    row = lambda a: a.reshape(1, -1).astype(F32)
    wts = dict(
        width=width, rwkv_in=rwkv_in, n_u=n_u, n_h=n_h,
        norm1_g=row(norm1_g[layer]), w_all=w_all,
        qg=row(jnp.tile(sb_q_g[layer], reps)), kg=row(jnp.tile(sb_k_g[layer], reps)),
        sb_o_g=row(sb_o_g[layer]),
        mu=jnp.pad(row(rwkv_mu[layer]), ((0, 0), (0, uw - rwkv_in))),
        w0=row(rwkv_w0[layer]), a0=row(rwkv_a0[layer]),
        k_k=row(rwkv_k_k[layer]), k_a=row(rwkv_k_a[layer]),
        w2p=jnp.pad(rwkv_w2[layer], ((0, aaa_lora), (0, 0))).astype(BF16),
        a2p=jnp.pad(rwkv_a2[layer], ((decay_lora, 0), (0, 0))).astype(BF16),
        g2p=jnp.pad(rwkv_g2[layer], ((0, tail_w - LANES - gate_lora), (0, 0))).astype(BF16),
        r_k=row(rwkv_r_k[layer]), ln_g=row(rwkv_ln_g[layer]), ln_b=row(rwkv_ln_b[layer]),
        w_out=w_out[layer].astype(BF16), norm2_g=row(norm2_g[layer]),
        rw=jnp.concatenate(_split_bf16(jnp.pad(router_w[layer], ((0, 0), (0, LANES - n_exp))), 2), axis=1),
        rb=jnp.pad(row(router_b[layer]), ((0, 0), (0, LANES - n_exp)), constant_values=-jnp.inf),
    )

    bp, tp, _ = x_prompt.shape
    bs, ts, _ = x_sample.shape
    heads = width // HEAD_DIM
    s0_prompt = jnp.zeros((bp, heads, HEAD_DIM, HEAD_DIM), F32)
    shift0_prompt = jnp.zeros((bp, rwkv_in), F32)
    grp_p = _mixers(x_prompt, None, s0_prompt, shift0_prompt, wts)
    grp_s = _mixers(x_sample, (cache_sb_k[layer], cache_sb_v[layer]), state_rwkv[layer],
                    state_shift[layer], wts)

    n_p, n_s = bp * tp, bs * ts
    n_tok = n_p + n_s
    h2 = jnp.concatenate([grp_p[1], grp_s[1]], axis=0)
    experts = jnp.concatenate([grp_p[2], grp_s[2]], axis=0)
    n_blocks = -(-(n_tok * TOP_K) // MOE_BLOCK) + n_exp
    dest, block_e, n_valid, n_used = _route(experts, n_exp, n_blocks)
    xbuf, w1b, w2b = _dispatch(h2, dest, n_valid, moe_w1[layer], moe_w2[layer])
    ybuf = _moe_ffn(xbuf, block_e, n_valid, n_used, w1b, moe_b1[layer], w2b, moe_b2[layer])
    y_p = _combine(grp_p[0], grp_p[3], dest[:n_p * TOP_K], ybuf).reshape(bp, tp, d)
    y_s = _combine(grp_s[0], grp_s[3], dest[n_p * TOP_K:], ybuf).reshape(bs, ts, d)

    lead = lambda a: a[None]
    return (y_p, y_s,
            lead(grp_p[4]), lead(grp_p[5]), lead(grp_p[6]), lead(grp_p[7]),
            lead(grp_s[4]), lead(grp_s[5]), lead(grp_s[6]), lead(grp_s[7]))
```
